```python
import math
import jax, jax.numpy as jnp
from jax import lax
import numpy as np

D_MODEL = 1024
BATCH = 1
SEQ = 16384
DEPTH = 1

HEAD_DIM = 64
GRID_W = 64
Q_BLOCK = 128
EPS = 1e-6
ROPE_THETA = 10000.0
AX_DIM = HEAD_DIM // 2
A_HEADS = 8
A_KV_HEADS = 2
A_GROUP = A_HEADS // A_KV_HEADS
B_HEADS = 4
B_VDIM = 2 * HEAD_DIM
A_Q_COLS = A_HEADS * HEAD_DIM
A_KV_COLS = A_KV_HEADS * HEAD_DIM
B_QK_COLS = B_HEADS * 2 * HEAD_DIM
B_V_COLS = B_HEADS * B_VDIM
IN_COLS = A_Q_COLS + 2 * A_KV_COLS + 2 * B_QK_COLS + B_V_COLS
MIX_WIDTH = A_HEADS * HEAD_DIM + B_HEADS * B_VDIM
D_FF = 2816
LAMBDA_STD = 0.1

kernel_name = "hybrid_gqa_axialrope_diffattn_alibi_macaron"


def rms_norm(x, g):
    xf = x.astype(jnp.float32)
    y = xf * lax.rsqrt(jnp.mean(xf * xf, axis=-1, keepdims=True) + EPS)
    return (y * g.astype(jnp.float32)).astype(x.dtype)


def swiglu(h, w_gate, w_up, w_down):
    return (jax.nn.silu(h @ w_gate) * (h @ w_up)) @ w_down


def rotate(part, cos, sin):
    x1, x2 = part[..., : AX_DIM // 2], part[..., AX_DIM // 2:]
    c = cos[None, :, None, :]
    s = sin[None, :, None, :]
    return jnp.concatenate([x1 * c - x2 * s, x2 * c + x1 * s], axis=-1)


def axial_rope(x, cos_r, sin_r, cos_c, sin_c):
    xf = x.astype(jnp.float32)
    out = jnp.concatenate([rotate(xf[..., :AX_DIM], cos_r, sin_r),
                           rotate(xf[..., AX_DIM:], cos_c, sin_c)], axis=-1)
    return out.astype(x.dtype)


def gqa_axial_attention(q, k, v, q_norm, k_norm, rope):
    B, S = q.shape[0], q.shape[1]
    nb = S // Q_BLOCK
    q = axial_rope(rms_norm(q, q_norm), *rope)
    k = axial_rope(rms_norm(k, k_norm), *rope)
    scale = 1.0 / math.sqrt(HEAD_DIM)
    qb = q.reshape(B, nb, Q_BLOCK, A_KV_HEADS, A_GROUP, HEAD_DIM).transpose(1, 0, 2, 3, 4, 5)

    def block(qblk):
        s = jnp.einsum('bqgrd,bkgd->bgrqk', qblk, k).astype(jnp.float32) * scale
        p = jax.nn.softmax(s, axis=-1).astype(v.dtype)
        return jnp.einsum('bgrqk,bkgd->bqgrd', p, v)

    o = lax.map(block, qb)
    return o.transpose(1, 0, 2, 3, 4, 5).reshape(B, S, A_HEADS * HEAD_DIM)


def diff_alibi_attention(q, k, v, q_norm, k_norm, lq1, lk1, lq2, lk2, subln, lambda_init):
    B, S = q.shape[0], q.shape[1]
    nb = S // Q_BLOCK
    q = rms_norm(q, q_norm)
    k = rms_norm(k, k_norm)
    scale = 1.0 / math.sqrt(HEAD_DIM)
    lam = (jnp.exp(jnp.sum(lq1.astype(jnp.float32) * lk1.astype(jnp.float32), axis=-1))
           - jnp.exp(jnp.sum(lq2.astype(jnp.float32) * lk2.astype(jnp.float32), axis=-1))
           + lambda_init)
    slopes = 2.0 ** (-8.0 * (jnp.arange(B_HEADS, dtype=jnp.float32) + 1.0) / B_HEADS)
    k_pos = jnp.arange(S, dtype=jnp.int32)
    qb = q.reshape(B, nb, Q_BLOCK, B_HEADS, 2, HEAD_DIM).transpose(1, 0, 2, 3, 4, 5)
    starts = jnp.arange(nb, dtype=jnp.int32) * Q_BLOCK

    def block(args):
        qblk, start = args
        s = jnp.einsum('bqhcd,bkhcd->bhcqk', qblk, k).astype(jnp.float32) * scale
        q_pos = start + jnp.arange(Q_BLOCK, dtype=jnp.int32)
        dist = jnp.abs(q_pos[:, None] - k_pos[None, :]).astype(jnp.float32)
        s = s - slopes[None, :, None, None, None] * dist[None, None, None]
        p = jax.nn.softmax(s, axis=-1)
        attn = (p[:, :, 0] - lam[None, :, None, None] * p[:, :, 1]).astype(v.dtype)
        return jnp.einsum('bhqk,bkhe->bqhe', attn, v)

    o = lax.map(block, (qb, starts))
    o = o.transpose(1, 0, 2, 3, 4).reshape(B, S, B_HEADS, B_VDIM)
    o = rms_norm(o, subln) * (1.0 - lambda_init)
    return o.reshape(B, S, B_HEADS * B_VDIM)


def setup_inputs(seed: int = 0) -> dict:
    key = jax.random.key(seed)
    ks = jax.random.split(key, 24)
    f32 = jnp.float32

    def w(k, shape, fan_in):
        return jax.random.normal(k, shape, f32) * (fan_in ** -0.5)

    def gain(k, shape):
        return 1.0 + 0.01 * jax.random.normal(k, shape, f32)

    L = DEPTH
    return {
        "x": jax.random.normal(ks[0], (BATCH, SEQ, D_MODEL), f32),
        "ffn1_norm": gain(ks[1], (L, D_MODEL)),
        "ffn1_w_gate": w(ks[2], (L, D_MODEL, D_FF), D_MODEL),
        "ffn1_w_up": w(ks[3], (L, D_MODEL, D_FF), D_MODEL),
        "ffn1_w_down": w(ks[4], (L, D_FF, D_MODEL), D_FF),
        "attn_norm": gain(ks[5], (L, D_MODEL)),
        "w_in": w(ks[6], (L, D_MODEL, IN_COLS), D_MODEL),
        "a_q_norm": gain(ks[7], (L, HEAD_DIM)),
        "a_k_norm": gain(ks[8], (L, HEAD_DIM)),
        "b_q_norm": gain(ks[9], (L, HEAD_DIM)),
        "b_k_norm": gain(ks[10], (L, HEAD_DIM)),
        "lambda_q1": LAMBDA_STD * jax.random.normal(ks[11], (L, B_HEADS, HEAD_DIM), f32),
        "lambda_k1": LAMBDA_STD * jax.random.normal(ks[12], (L, B_HEADS, HEAD_DIM), f32),
        "lambda_q2": LAMBDA_STD * jax.random.normal(ks[13], (L, B_HEADS, HEAD_DIM), f32),
        "lambda_k2": LAMBDA_STD * jax.random.normal(ks[14], (L, B_HEADS, HEAD_DIM), f32),
        "b_subln": gain(ks[15], (L, B_VDIM)),
        "w_out": w(ks[16], (L, MIX_WIDTH, D_MODEL), MIX_WIDTH),
        "ffn2_norm": gain(ks[17], (L, D_MODEL)),
        "ffn2_w_gate": w(ks[18], (L, D_MODEL, D_FF), D_MODEL),
        "ffn2_w_up": w(ks[19], (L, D_MODEL, D_FF), D_MODEL),
        "ffn2_w_down": w(ks[20], (L, D_FF, D_MODEL), D_FF),
    }


def reference(x, ffn1_norm, ffn1_w_gate, ffn1_w_up, ffn1_w_down, attn_norm, w_in,
              a_q_norm, a_k_norm, b_q_norm, b_k_norm, lambda_q1, lambda_k1, lambda_q2, lambda_k2,
              b_subln, w_out, ffn2_norm, ffn2_w_gate, ffn2_w_up, ffn2_w_down):
    B, S, _ = x.shape
    ROWS = S // GRID_W
    row = jnp.broadcast_to(jnp.arange(ROWS, dtype=jnp.float32)[:, None], (ROWS, GRID_W)).reshape(-1)
    col = jnp.broadcast_to(jnp.arange(GRID_W, dtype=jnp.float32)[None, :], (ROWS, GRID_W)).reshape(-1)
    inv_freq = ROPE_THETA ** (-jnp.arange(0, AX_DIM, 2, dtype=jnp.float32) / AX_DIM)
    ang_r = row[:, None] * inv_freq[None, :]
    ang_c = col[:, None] * inv_freq[None, :]
    rope = (jnp.cos(ang_r), jnp.sin(ang_r), jnp.cos(ang_c), jnp.sin(ang_c))

    split_at = [A_Q_COLS, A_Q_COLS + A_KV_COLS, A_Q_COLS + 2 * A_KV_COLS,
                A_Q_COLS + 2 * A_KV_COLS + B_QK_COLS, A_Q_COLS + 2 * A_KV_COLS + 2 * B_QK_COLS]

    for l in range(DEPTH):
        lambda_init = 0.8 - 0.6 * math.exp(-0.3 * l)
        h = rms_norm(x, ffn1_norm[l])
        x = x + 0.5 * swiglu(h, ffn1_w_gate[l], ffn1_w_up[l], ffn1_w_down[l])
        h = rms_norm(x, attn_norm[l])
        proj = h @ w_in[l]
        a_q, a_k, a_v, b_q, b_k, b_v = jnp.split(proj, split_at, axis=-1)
        a_out = gqa_axial_attention(
            a_q.reshape(B, S, A_HEADS, HEAD_DIM),
            a_k.reshape(B, S, A_KV_HEADS, HEAD_DIM),
            a_v.reshape(B, S, A_KV_HEADS, HEAD_DIM),
            a_q_norm[l], a_k_norm[l], rope)
        b_out = diff_alibi_attention(
            b_q.reshape(B, S, B_HEADS, 2, HEAD_DIM),
            b_k.reshape(B, S, B_HEADS, 2, HEAD_DIM),
            b_v.reshape(B, S, B_HEADS, B_VDIM),
            b_q_norm[l], b_k_norm[l], lambda_q1[l], lambda_k1[l], lambda_q2[l], lambda_k2[l],
            b_subln[l], lambda_init)
        x = x + jnp.concatenate([a_out, b_out], axis=-1) @ w_out[l]
        h = rms_norm(x, ffn2_norm[l])
        x = x + 0.5 * swiglu(h, ffn2_w_gate[l], ffn2_w_up[l], ffn2_w_down[l])
    return x
```

```python
import functools
import math

import jax
import jax.numpy as jnp
from jax import lax
from jax.experimental import pallas as pl
from jax.experimental.pallas import tpu as pltpu

D_MODEL = 1024
HEAD_DIM = 64
GRID_W = 64
EPS = 1e-6
ROPE_THETA = 10000.0
AX_DIM = HEAD_DIM // 2
A_HEADS = 8
A_KV_HEADS = 2
A_GROUP = A_HEADS // A_KV_HEADS
B_HEADS = 4
B_VDIM = 2 * HEAD_DIM
A_Q_COLS = A_HEADS * HEAD_DIM
A_KV_COLS = A_KV_HEADS * HEAD_DIM
B_QK_COLS = B_HEADS * 2 * HEAD_DIM
B_V_COLS = B_HEADS * B_VDIM
IN_COLS = A_Q_COLS + 2 * A_KV_COLS + 2 * B_QK_COLS + B_V_COLS
D_FF = 2816

LANES = 128
BF16_SUBLANES = 16
ROW_TILE = 512
KV_CHUNK = 512
A_Q_TOKENS = 128
B_Q_TOKENS = 256
FF_CHUNK = 1408
VMEM_LIMIT = 56 * 1024 * 1024
NEG_BIG = -1e30

_F32 = jnp.float32
_BF16 = jnp.bfloat16


def _dot(a, b):
    return jnp.dot(a, b, preferred_element_type=_F32)


def _rms_rows(x, gain):
    return x * lax.rsqrt(jnp.mean(x * x, axis=-1, keepdims=True) + EPS) * gain


def _swiglu_half_step(x, gain, wg_ref, wu_ref, wd_ref):
    h = _rms_rows(x, gain).astype(_BF16)
    acc = jnp.zeros(x.shape, _F32)
    for c in range(D_FF // FF_CHUNK):
        cols = slice(c * FF_CHUNK, (c + 1) * FF_CHUNK)
        g = _dot(h, wg_ref[:, cols])
        u = _dot(h, wu_ref[:, cols])
        a = (g / (1.0 + jnp.exp(-g)) * u).astype(_BF16)
        acc = acc + _dot(a, wd_ref[cols, :])
    return x + 0.5 * acc


def _ffn_body(x_ref, gain_ref, wg_ref, wu_ref, wd_ref, o_ref):
    o_ref[...] = _swiglu_half_step(x_ref[...], gain_ref[...], wg_ref, wu_ref, wd_ref)


def _prep_body(x_ref, gain_ref, w_in_ref, cos_ref, sin_ref, gqa_ref, gka_ref, gqb_ref, gkb_ref, seg_ref,
               qta_ref, ka_ref, vta_ref, qtb_ref, kb_ref, vtb_ref):
    h = _rms_rows(x_ref[...], gain_ref[...]).astype(_BF16)
    proj = _dot(h, w_in_ref[...])
    seg = seg_ref[...]
    scale = 1.0 / math.sqrt(HEAD_DIM)

    def block(col):
        return proj[:, col:col + LANES]

    def qk_norm(blk, gain):
        sq = blk * blk
        hi = sq.astype(_BF16)
        lo = (sq - hi.astype(_F32)).astype(_BF16)
        ms = _dot(hi, seg) + _dot(lo, seg)
        return blk * lax.rsqrt(ms + EPS) * gain

    lane = lax.broadcasted_iota(jnp.int32, (proj.shape[0], LANES), 1)
    first_half = (lane % AX_DIM) < (AX_DIM // 2)
    cos = cos_ref[...]
    sin = sin_ref[...]

    def rope(blk):
        partner = jnp.where(first_half, pltpu.roll(blk, LANES - AX_DIM // 2, 1), pltpu.roll(blk, AX_DIM // 2, 1))
        return blk * cos + partner * sin

    for b in range(A_Q_COLS // LANES):
        q = rope(qk_norm(block(b * LANES), gqa_ref[...])) * scale
        qta_ref[b * LANES:(b + 1) * LANES, :] = q.T.astype(_BF16)
    k = rope(qk_norm(block(A_Q_COLS), gka_ref[...])).astype(_BF16)
    vt = block(A_Q_COLS + A_KV_COLS).T.astype(_BF16)
    ones_a = jnp.ones((BF16_SUBLANES, proj.shape[0]), _BF16)
    for g in range(A_KV_HEADS):
        ka_ref[g] = k[:, g * HEAD_DIM:(g + 1) * HEAD_DIM]
        vta_ref[g, 0, 0:HEAD_DIM, :] = vt[g * HEAD_DIM:(g + 1) * HEAD_DIM, :]
        vta_ref[g, 0, HEAD_DIM:HEAD_DIM + BF16_SUBLANES, :] = ones_a
    b_q0 = A_Q_COLS + 2 * A_KV_COLS
    b_k0 = b_q0 + B_QK_COLS
    b_v0 = b_k0 + B_QK_COLS
    for hd in range(B_HEADS):
        q = qk_norm(block(b_q0 + hd * LANES), gqb_ref[...]) * scale
        qtb_ref[hd * LANES:(hd + 1) * LANES, :] = q.T.astype(_BF16)
        kb_ref[:, hd * LANES:(hd + 1) * LANES] = qk_norm(block(b_k0 + hd * LANES), gkb_ref[...]).astype(_BF16)
        vtb_ref[hd, 0, 0:B_VDIM, :] = block(b_v0 + hd * LANES).T.astype(_BF16)
        vtb_ref[hd, 0, B_VDIM:B_VDIM + BF16_SUBLANES, :] = ones_a


def _flash_columns(w, k_ref, vt_ref, bias_fn):
    n = w.shape[1]
    rows = vt_ref.shape[1]

    def step(j, carry):
        m, acc = carry
        k = k_ref[pl.ds(pl.multiple_of(j * KV_CHUNK, KV_CHUNK), KV_CHUNK), :]
        s = _dot(k, w)
        if bias_fn is not None:
            s = s - bias_fn(j)
        m_new = jnp.maximum(m, jnp.max(s, axis=0, keepdims=True))
        p = jnp.exp(s - m_new).astype(_BF16)
        acc = jnp.exp(m - m_new) * acc + _dot(vt_ref[j], p)
        return m_new, acc

    init = (jnp.full((1, n), NEG_BIG, _F32), jnp.zeros((rows, n), _F32))
    _, acc = lax.fori_loop(0, vt_ref.shape[0], step, init)
    return acc


def _attn_a_body(qt_ref, k_ref, vt_ref, o_ref):
    tq = qt_ref.shape[1]
    w = jnp.concatenate([qt_ref[r * HEAD_DIM:(r + 1) * HEAD_DIM, :] for r in range(A_GROUP)], axis=1)
    acc = _flash_columns(w, k_ref, vt_ref, None)
    out = acc[0:HEAD_DIM, :] / acc[HEAD_DIM:HEAD_DIM + 1, :]
    for r in range(A_GROUP):
        o_ref[r * HEAD_DIM:(r + 1) * HEAD_DIM, :] = out[:, r * tq:(r + 1) * tq].astype(o_ref.dtype)


def _attn_b_body(qt_ref, k_ref, vt_ref, slope_ref, lq1_ref, lk1_ref, lq2_ref, lk2_ref, subln_ref, o_ref, *,
                 lambda_init):
    tq = qt_ref.shape[1]
    zeros = jnp.zeros((HEAD_DIM, tq), _BF16)
    w = jnp.concatenate([jnp.concatenate([qt_ref[0:HEAD_DIM, :], zeros], axis=1),
                         jnp.concatenate([zeros, qt_ref[HEAD_DIM:2 * HEAD_DIM, :]], axis=1)], axis=0)
    slope = slope_ref[...]
    q_pos = pl.program_id(1) * tq + lax.broadcasted_iota(jnp.int32, (KV_CHUNK, tq), 1)
    k_off = lax.broadcasted_iota(jnp.int32, (KV_CHUNK, tq), 0)

    def bias_fn(j):
        dist = jnp.abs(q_pos - (k_off + j * KV_CHUNK)).astype(_F32)
        bias = slope * dist
        return jnp.concatenate([bias, bias], axis=1)

    acc = _flash_columns(w, k_ref, vt_ref, bias_fn)
    lam = (jnp.exp(jnp.sum(lq1_ref[...] * lk1_ref[...], axis=-1, keepdims=True))
           - jnp.exp(jnp.sum(lq2_ref[...] * lk2_ref[...], axis=-1, keepdims=True)) + lambda_init)
    o0 = acc[0:B_VDIM, 0:tq] / acc[B_VDIM:B_VDIM + 1, 0:tq]
    o1 = acc[0:B_VDIM, tq:2 * tq] / acc[B_VDIM:B_VDIM + 1, tq:2 * tq]
    o = o0 - lam * o1
    o = o * lax.rsqrt(jnp.mean(o * o, axis=0, keepdims=True) + EPS) * subln_ref[...] * (1.0 - lambda_init)
    o_ref[...] = o.astype(o_ref.dtype)


def _out_body(x_ref, at_ref, bt_ref, w_out_ref, gain_ref, wg_ref, wu_ref, wd_ref, o_ref):
    tn = (((0,), (0,)), ((), ()))
    half = at_ref.shape[0]
    y = (lax.dot_general(at_ref[...], w_out_ref[0:half, :], tn, preferred_element_type=_F32)
         + lax.dot_general(bt_ref[...], w_out_ref[half:2 * half, :], tn, preferred_element_type=_F32))
    x = x_ref[...] + y
    o_ref[...] = _swiglu_half_step(x, gain_ref[...], wg_ref, wu_ref, wd_ref)


def _resident(shape):
    return pl.BlockSpec(shape, lambda *_: (0,) * len(shape), pipeline_mode=pl.Buffered(1))


def _params(n_axes):
    return pltpu.CompilerParams(dimension_semantics=("arbitrary",) * n_axes, vmem_limit_bytes=VMEM_LIMIT)


def _ffn_call(x2d, gain, wg, wu, wd):
    s = x2d.shape[0]
    row = pl.BlockSpec((ROW_TILE, D_MODEL), lambda i: (i, 0))
    return pl.pallas_call(
        _ffn_body, name="ffn_half_step",
        grid=(s // ROW_TILE,),
        in_specs=[row, _resident((1, D_MODEL)), _resident(wg.shape), _resident(wu.shape), _resident(wd.shape)],
        out_specs=row,
        out_shape=jax.ShapeDtypeStruct((s, D_MODEL), _F32),
        compiler_params=_params(1),
    )(x2d, gain, wg, wu, wd)


def _prep_call(x2d, gain, w_in, cos, sin, gqa, gka, gqb, gkb, seg):
    s = x2d.shape[0]
    nk = s // KV_CHUNK
    assert ROW_TILE == KV_CHUNK
    row = lambda width: pl.BlockSpec((ROW_TILE, width), lambda i: (i, 0))
    col = lambda height: pl.BlockSpec((height, ROW_TILE), lambda i: (0, i))
    vec = _resident((1, LANES))
    out_shape = (
        jax.ShapeDtypeStruct((A_Q_COLS, s), _BF16),
        jax.ShapeDtypeStruct((A_KV_HEADS, s, HEAD_DIM), _BF16),
        jax.ShapeDtypeStruct((A_KV_HEADS, nk, HEAD_DIM + BF16_SUBLANES, KV_CHUNK), _BF16),
        jax.ShapeDtypeStruct((B_QK_COLS, s), _BF16),
        jax.ShapeDtypeStruct((s, B_QK_COLS), _BF16),
        jax.ShapeDtypeStruct((B_HEADS, nk, B_VDIM + BF16_SUBLANES, KV_CHUNK), _BF16),
    )
    out_specs = (
        col(A_Q_COLS),
        pl.BlockSpec((A_KV_HEADS, ROW_TILE, HEAD_DIM), lambda i: (0, i, 0)),
        pl.BlockSpec((A_KV_HEADS, 1, HEAD_DIM + BF16_SUBLANES, KV_CHUNK), lambda i: (0, i, 0, 0)),
        col(B_QK_COLS),
        row(B_QK_COLS),
        pl.BlockSpec((B_HEADS, 1, B_VDIM + BF16_SUBLANES, KV_CHUNK), lambda i: (0, i, 0, 0)),
    )
    return pl.pallas_call(
        _prep_body, name="in_proj_prep",
        grid=(s // ROW_TILE,),
        in_specs=[row(D_MODEL), _resident((1, D_MODEL)), _resident(w_in.shape), row(LANES), row(LANES),
                  vec, vec, vec, vec, _resident((LANES, LANES))],
        out_specs=out_specs,
        out_shape=out_shape,
        compiler_params=_params(1),
    )(x2d, gain, w_in, cos, sin, gqa, gka, gqb, gkb, seg)


def _attn_a_call(qta, ka, vta):
    s = qta.shape[1]
    nk = s // KV_CHUNK
    rows = A_GROUP * HEAD_DIM
    return pl.pallas_call(
        _attn_a_body, name="attn_gqa",
        grid=(A_KV_HEADS, s // A_Q_TOKENS),
        in_specs=[pl.BlockSpec((rows, A_Q_TOKENS), lambda g, i: (g, i)),
                  pl.BlockSpec((None, s, HEAD_DIM), lambda g, i: (g, 0, 0)),
                  pl.BlockSpec((None, nk, HEAD_DIM + BF16_SUBLANES, KV_CHUNK), lambda g, i: (g, 0, 0, 0))],
        out_specs=pl.BlockSpec((rows, A_Q_TOKENS), lambda g, i: (g, i)),
        out_shape=jax.ShapeDtypeStruct((A_Q_COLS, s), _BF16),
        compiler_params=_params(2),
    )(qta, ka, vta)


def _attn_b_call(qtb, kb, vtb, slopes, lq1, lk1, lq2, lk2, subln, lambda_init):
    s = qtb.shape[1]
    nk = s // KV_CHUNK
    per_head = lambda width: pl.BlockSpec((None, 1, width), lambda h, i: (h, 0, 0))
    return pl.pallas_call(
        functools.partial(_attn_b_body, lambda_init=lambda_init), name="attn_diff",
        grid=(B_HEADS, s // B_Q_TOKENS),
        in_specs=[pl.BlockSpec((2 * HEAD_DIM, B_Q_TOKENS), lambda h, i: (h, i)),
                  pl.BlockSpec((s, 2 * HEAD_DIM), lambda h, i: (0, h)),
                  pl.BlockSpec((None, nk, B_VDIM + BF16_SUBLANES, KV_CHUNK), lambda h, i: (h, 0, 0, 0)),
                  per_head(1), per_head(HEAD_DIM), per_head(HEAD_DIM), per_head(HEAD_DIM), per_head(HEAD_DIM),
                  pl.BlockSpec((B_VDIM, 1), lambda h, i: (0, 0))],
        out_specs=pl.BlockSpec((B_VDIM, B_Q_TOKENS), lambda h, i: (h, i)),
        out_shape=jax.ShapeDtypeStruct((B_V_COLS, s), _BF16),
        compiler_params=_params(2),
    )(qtb, kb, vtb, slopes, lq1, lk1, lq2, lk2, subln)


def _out_call(x2d, at, bt, w_out, gain, wg, wu, wd):
    s = x2d.shape[0]
    row = pl.BlockSpec((ROW_TILE, D_MODEL), lambda i: (i, 0))
    col = pl.BlockSpec((at.shape[0], ROW_TILE), lambda i: (0, i))
    return pl.pallas_call(
        _out_body, name="out_proj_ffn",
        grid=(s // ROW_TILE,),
        in_specs=[row, col, col, _resident(w_out.shape), _resident((1, D_MODEL)),
                  _resident(wg.shape), _resident(wu.shape), _resident(wd.shape)],
        out_specs=row,
        out_shape=jax.ShapeDtypeStruct((s, D_MODEL), _F32),
        compiler_params=_params(1),
    )(x2d, at, bt, w_out, gain, wg, wu, wd)


def _rope_tables(s):
    rows = s // GRID_W
    row = jnp.broadcast_to(jnp.arange(rows, dtype=_F32)[:, None], (rows, GRID_W)).reshape(-1)
    colp = jnp.broadcast_to(jnp.arange(GRID_W, dtype=_F32)[None, :], (rows, GRID_W)).reshape(-1)
    inv_freq = ROPE_THETA ** (-jnp.arange(0, AX_DIM, 2, dtype=_F32) / AX_DIM)
    ang_r = row[:, None] * inv_freq[None, :]
    ang_c = colp[:, None] * inv_freq[None, :]
    cos = jnp.concatenate([jnp.cos(ang_r)] * 2 + [jnp.cos(ang_c)] * 2, axis=-1)
    sin = jnp.concatenate([-jnp.sin(ang_r), jnp.sin(ang_r), -jnp.sin(ang_c), jnp.sin(ang_c)], axis=-1)
    return jnp.tile(cos, (1, LANES // HEAD_DIM)), jnp.tile(sin, (1, LANES // HEAD_DIM))


def kernel(x, ffn1_norm, ffn1_w_gate, ffn1_w_up, ffn1_w_down, attn_norm, w_in, a_q_norm, a_k_norm, b_q_norm, b_k_norm, lambda_q1, lambda_k1, lambda_q2, lambda_k2, b_subln, w_out, ffn2_norm, ffn2_w_gate, ffn2_w_up, ffn2_w_down):
    bsz, s, _ = x.shape
    depth = w_in.shape[0]
    assert s % ROW_TILE == 0 and s % KV_CHUNK == 0 and s % A_Q_TOKENS == 0 and s % B_Q_TOKENS == 0
    cos, sin = _rope_tables(s)
    lane_chunk = jnp.arange(LANES) // HEAD_DIM
    seg = ((lane_chunk[:, None] == lane_chunk[None, :]).astype(_F32) / HEAD_DIM).astype(_BF16)
    slopes = (2.0 ** (-8.0 * (jnp.arange(B_HEADS, dtype=_F32) + 1.0) / B_HEADS)).reshape(B_HEADS, 1, 1)
    lane_gain = lambda g: jnp.tile(g, LANES // HEAD_DIM).reshape(1, LANES)
    bf = lambda w: w.astype(_BF16)

    outs = []
    for b in range(bsz):
        xb = x[b]
        for l in range(depth):
            lambda_init = 0.8 - 0.6 * math.exp(-0.3 * l)
            xb = _ffn_call(xb, ffn1_norm[l].reshape(1, D_MODEL), bf(ffn1_w_gate[l]), bf(ffn1_w_up[l]), bf(ffn1_w_down[l]))
            qta, ka, vta, qtb, kb, vtb = _prep_call(
                xb, attn_norm[l].reshape(1, D_MODEL), bf(w_in[l]), cos, sin,
                lane_gain(a_q_norm[l]), lane_gain(a_k_norm[l]), lane_gain(b_q_norm[l]), lane_gain(b_k_norm[l]), seg)
            at = _attn_a_call(qta, ka, vta)
            per_head = lambda p: p[l].reshape(B_HEADS, 1, HEAD_DIM)
            bt = _attn_b_call(qtb, kb, vtb, slopes, per_head(lambda_q1), per_head(lambda_k1), per_head(lambda_q2),
                              per_head(lambda_k2), b_subln[l].reshape(B_VDIM, 1), lambda_init)
            xb = _out_call(xb, at, bt, bf(w_out[l]), ffn2_norm[l].reshape(1, D_MODEL),
                           bf(ffn2_w_gate[l]), bf(ffn2_w_up[l]), bf(ffn2_w_down[l]))
        outs.append(xb)
    return jnp.stack(outs, axis=0)
```

```python
import functools
import math

import jax
import jax.numpy as jnp
from jax import lax
from jax.experimental import pallas as pl
from jax.experimental.pallas import tpu as pltpu

D_MODEL = 1024
HEAD_DIM = 64
GRID_W = 64
EPS = 1e-6
ROPE_THETA = 10000.0
AX_DIM = HEAD_DIM // 2
A_HEADS = 8
A_KV_HEADS = 2
A_GROUP = A_HEADS // A_KV_HEADS
B_HEADS = 4
B_VDIM = 2 * HEAD_DIM
A_Q_COLS = A_HEADS * HEAD_DIM
A_KV_COLS = A_KV_HEADS * HEAD_DIM
B_QK_COLS = B_HEADS * 2 * HEAD_DIM
B_V_COLS = B_HEADS * B_VDIM
IN_COLS = A_Q_COLS + 2 * A_KV_COLS + 2 * B_QK_COLS + B_V_COLS
D_FF = 2816

LANES = 128
BF16_SUBLANES = 16
ROW_TILE = 512
KV_CHUNK = 512
A_Q_TOKENS = 128
B_Q_TOKENS = 256
FF_CHUNK = 1408
VMEM_LIMIT = 56 * 1024 * 1024
NEG_BIG = -1e30

_F32 = jnp.float32
_BF16 = jnp.bfloat16


def _dot(a, b):
    return jnp.dot(a, b, preferred_element_type=_F32)


def _rms_rows(x, gain):
    return x * lax.rsqrt(jnp.mean(x * x, axis=-1, keepdims=True) + EPS) * gain


def _swiglu_half_step(x, gain, wg_ref, wu_ref, wd_ref):
    h = _rms_rows(x, gain).astype(_BF16)
    acc = jnp.zeros(x.shape, _F32)
    for c in range(D_FF // FF_CHUNK):
        cols = slice(c * FF_CHUNK, (c + 1) * FF_CHUNK)
        g = _dot(h, wg_ref[:, cols])
        u = _dot(h, wu_ref[:, cols])
        a = (g / (1.0 + jnp.exp(-g)) * u).astype(_BF16)
        acc = acc + _dot(a, wd_ref[cols, :])
    return x + 0.5 * acc


def _ffn_body(x_ref, gain_ref, wg_ref, wu_ref, wd_ref, o_ref):
    o_ref[...] = _swiglu_half_step(x_ref[...], gain_ref[...], wg_ref, wu_ref, wd_ref)


def _prep_body(x_ref, gain_ref, w_in_ref, cos_ref, sin_ref, gqa_ref, gka_ref, gqb_ref, gkb_ref, seg_ref,
               qta_ref, ka_ref, vta_ref, qtb_ref, kb_ref, vtb_ref):
    h = _rms_rows(x_ref[...], gain_ref[...]).astype(_BF16)
    proj = _dot(h, w_in_ref[...])
    seg = seg_ref[...]
    scale = 1.0 / math.sqrt(HEAD_DIM)

    def block(col):
        return proj[:, col:col + LANES]

    def qk_norm(blk, gain):
        sq = blk * blk
        hi = sq.astype(_BF16)
        lo = (sq - hi.astype(_F32)).astype(_BF16)
        ms = _dot(hi, seg) + _dot(lo, seg)
        return blk * lax.rsqrt(ms + EPS) * gain

    lane = lax.broadcasted_iota(jnp.int32, (proj.shape[0], LANES), 1)
    first_half = (lane % AX_DIM) < (AX_DIM // 2)
    cos = cos_ref[...]
    sin = sin_ref[...]

    def rope(blk):
        partner = jnp.where(first_half, pltpu.roll(blk, LANES - AX_DIM // 2, 1), pltpu.roll(blk, AX_DIM // 2, 1))
        return blk * cos + partner * sin

    for b in range(A_Q_COLS // LANES):
        q = rope(qk_norm(block(b * LANES), gqa_ref[...])) * scale
        qta_ref[b * LANES:(b + 1) * LANES, :] = q.T.astype(_BF16)
    k = rope(qk_norm(block(A_Q_COLS), gka_ref[...])).astype(_BF16)
    vt = block(A_Q_COLS + A_KV_COLS).T.astype(_BF16)
    ones_a = jnp.ones((BF16_SUBLANES, proj.shape[0]), _BF16)
    for g in range(A_KV_HEADS):
        ka_ref[g] = k[:, g * HEAD_DIM:(g + 1) * HEAD_DIM]
        vta_ref[g, 0, 0:HEAD_DIM, :] = vt[g * HEAD_DIM:(g + 1) * HEAD_DIM, :]
        vta_ref[g, 0, HEAD_DIM:HEAD_DIM + BF16_SUBLANES, :] = ones_a
    b_q0 = A_Q_COLS + 2 * A_KV_COLS
    b_k0 = b_q0 + B_QK_COLS
    b_v0 = b_k0 + B_QK_COLS
    for hd in range(B_HEADS):
        q = qk_norm(block(b_q0 + hd * LANES), gqb_ref[...]) * scale
        qtb_ref[hd * LANES:(hd + 1) * LANES, :] = q.T.astype(_BF16)
        kb_ref[:, hd * LANES:(hd + 1) * LANES] = qk_norm(block(b_k0 + hd * LANES), gkb_ref[...]).astype(_BF16)
        vtb_ref[hd, 0, 0:B_VDIM, :] = block(b_v0 + hd * LANES).T.astype(_BF16)
        vtb_ref[hd, 0, B_VDIM:B_VDIM + BF16_SUBLANES, :] = ones_a


def _flash_columns(w, k_ref, vt_ref, s_scr, p_scr, bias_fn):
    n = w.shape[1]
    rows = vt_ref.shape[1]
    nk = vt_ref.shape[0]
    assert nk % 2 == 0

    def scores(j, slot):
        k = k_ref[pl.ds(pl.multiple_of(j * KV_CHUNK, KV_CHUNK), KV_CHUNK), :]
        s = _dot(k, w)
        if bias_fn is not None:
            s = s - bias_fn(j)
        s_scr[slot] = s
        return jnp.max(s, axis=0, keepdims=True)

    def weights(slot, m, cmax):
        m_new = jnp.maximum(m, cmax)
        p_scr[slot] = jnp.exp(s_scr[slot] - m_new).astype(_BF16)
        return m_new, jnp.exp(m - m_new)

    def values(j, slot, acc, alpha):
        return alpha * acc + _dot(vt_ref[j], p_scr[slot])

    m = jnp.full((1, n), NEG_BIG, _F32)
    acc = jnp.zeros((rows, n), _F32)
    cmax0 = scores(0, 0)
    cmax1 = scores(1, 1)
    m, alpha = weights(0, m, cmax0)

    def body(i, carry):
        m, alpha, cmax1, acc = carry
        j = 2 * i + 1
        cmax0 = scores(j + 1, 0)
        acc = values(j - 1, 0, acc, alpha)
        m, alpha = weights(1, m, cmax1)
        cmax1 = scores(j + 2, 1)
        acc = values(j, 1, acc, alpha)
        m, alpha = weights(0, m, cmax0)
        return m, alpha, cmax1, acc

    m, alpha, cmax1, acc = lax.fori_loop(0, nk // 2 - 1, body, (m, alpha, cmax1, acc))
    acc = values(nk - 2, 0, acc, alpha)
    m, alpha = weights(1, m, cmax1)
    return values(nk - 1, 1, acc, alpha)


def _attn_a_body(qt_ref, k_ref, vt_ref, o_ref, s_scr, p_scr):
    tq = qt_ref.shape[1]
    w = jnp.concatenate([qt_ref[r * HEAD_DIM:(r + 1) * HEAD_DIM, :] for r in range(A_GROUP)], axis=1)
    acc = _flash_columns(w, k_ref, vt_ref, s_scr, p_scr, None)
    out = acc[0:HEAD_DIM, :] / acc[HEAD_DIM:HEAD_DIM + 1, :]
    for r in range(A_GROUP):
        o_ref[r * HEAD_DIM:(r + 1) * HEAD_DIM, :] = out[:, r * tq:(r + 1) * tq].astype(o_ref.dtype)


def _attn_b_body(qt_ref, k_ref, vt_ref, slope_ref, lq1_ref, lk1_ref, lq2_ref, lk2_ref, subln_ref, o_ref,
                 s_scr, p_scr, *, lambda_init):
    tq = qt_ref.shape[1]
    zeros = jnp.zeros((HEAD_DIM, tq), _BF16)
    w = jnp.concatenate([jnp.concatenate([qt_ref[0:HEAD_DIM, :], zeros], axis=1),
                         jnp.concatenate([zeros, qt_ref[HEAD_DIM:2 * HEAD_DIM, :]], axis=1)], axis=0)
    slope = slope_ref[...]
    q_pos = pl.program_id(1) * tq + lax.broadcasted_iota(jnp.int32, (KV_CHUNK, tq), 1)
    k_off = lax.broadcasted_iota(jnp.int32, (KV_CHUNK, tq), 0)

    def bias_fn(j):
        dist = jnp.abs(q_pos - (k_off + j * KV_CHUNK)).astype(_F32)
        bias = slope * dist
        return jnp.concatenate([bias, bias], axis=1)

    acc = _flash_columns(w, k_ref, vt_ref, s_scr, p_scr, bias_fn)
    lam = (jnp.exp(jnp.sum(lq1_ref[...] * lk1_ref[...], axis=-1, keepdims=True))
           - jnp.exp(jnp.sum(lq2_ref[...] * lk2_ref[...], axis=-1, keepdims=True)) + lambda_init)
    o0 = acc[0:B_VDIM, 0:tq] / acc[B_VDIM:B_VDIM + 1, 0:tq]
    o1 = acc[0:B_VDIM, tq:2 * tq] / acc[B_VDIM:B_VDIM + 1, tq:2 * tq]
    o = o0 - lam * o1
    o = o * lax.rsqrt(jnp.mean(o * o, axis=0, keepdims=True) + EPS) * subln_ref[...] * (1.0 - lambda_init)
    o_ref[...] = o.astype(o_ref.dtype)


def _out_body(x_ref, at_ref, bt_ref, w_out_ref, gain_ref, wg_ref, wu_ref, wd_ref, o_ref):
    tn = (((0,), (0,)), ((), ()))
    half = at_ref.shape[0]
    y = (lax.dot_general(at_ref[...], w_out_ref[0:half, :], tn, preferred_element_type=_F32)
         + lax.dot_general(bt_ref[...], w_out_ref[half:2 * half, :], tn, preferred_element_type=_F32))
    x = x_ref[...] + y
    o_ref[...] = _swiglu_half_step(x, gain_ref[...], wg_ref, wu_ref, wd_ref)


def _resident(shape):
    return pl.BlockSpec(shape, lambda *_: (0,) * len(shape), pipeline_mode=pl.Buffered(1))


def _attn_scratch(n):
    return [pltpu.VMEM((2, KV_CHUNK, n), _F32), pltpu.VMEM((2, KV_CHUNK, n), _BF16)]


def _params(n_axes):
    return pltpu.CompilerParams(dimension_semantics=("arbitrary",) * n_axes, vmem_limit_bytes=VMEM_LIMIT)


def _ffn_call(x2d, gain, wg, wu, wd):
    s = x2d.shape[0]
    row = pl.BlockSpec((ROW_TILE, D_MODEL), lambda i: (i, 0))
    return pl.pallas_call(
        _ffn_body, name="ffn_half_step",
        grid=(s // ROW_TILE,),
        in_specs=[row, _resident((1, D_MODEL)), _resident(wg.shape), _resident(wu.shape), _resident(wd.shape)],
        out_specs=row,
        out_shape=jax.ShapeDtypeStruct((s, D_MODEL), _F32),
        compiler_params=_params(1),
    )(x2d, gain, wg, wu, wd)


def _prep_call(x2d, gain, w_in, cos, sin, gqa, gka, gqb, gkb, seg):
    s = x2d.shape[0]
    nk = s // KV_CHUNK
    assert ROW_TILE == KV_CHUNK
    row = lambda width: pl.BlockSpec((ROW_TILE, width), lambda i: (i, 0))
    col = lambda height: pl.BlockSpec((height, ROW_TILE), lambda i: (0, i))
    vec = _resident((1, LANES))
    out_shape = (
        jax.ShapeDtypeStruct((A_Q_COLS, s), _BF16),
        jax.ShapeDtypeStruct((A_KV_HEADS, s, HEAD_DIM), _BF16),
        jax.ShapeDtypeStruct((A_KV_HEADS, nk, HEAD_DIM + BF16_SUBLANES, KV_CHUNK), _BF16),
        jax.ShapeDtypeStruct((B_QK_COLS, s), _BF16),
        jax.ShapeDtypeStruct((s, B_QK_COLS), _BF16),
        jax.ShapeDtypeStruct((B_HEADS, nk, B_VDIM + BF16_SUBLANES, KV_CHUNK), _BF16),
    )
    out_specs = (
        col(A_Q_COLS),
        pl.BlockSpec((A_KV_HEADS, ROW_TILE, HEAD_DIM), lambda i: (0, i, 0)),
        pl.BlockSpec((A_KV_HEADS, 1, HEAD_DIM + BF16_SUBLANES, KV_CHUNK), lambda i: (0, i, 0, 0)),
        col(B_QK_COLS),
        row(B_QK_COLS),
        pl.BlockSpec((B_HEADS, 1, B_VDIM + BF16_SUBLANES, KV_CHUNK), lambda i: (0, i, 0, 0)),
    )
    return pl.pallas_call(
        _prep_body, name="in_proj_prep",
        grid=(s // ROW_TILE,),
        in_specs=[row(D_MODEL), _resident((1, D_MODEL)), _resident(w_in.shape), row(LANES), row(LANES),
                  vec, vec, vec, vec, _resident((LANES, LANES))],
        out_specs=out_specs,
        out_shape=out_shape,
        compiler_params=_params(1),
    )(x2d, gain, w_in, cos, sin, gqa, gka, gqb, gkb, seg)


def _attn_a_call(qta, ka, vta):
    s = qta.shape[1]
    nk = s // KV_CHUNK
    rows = A_GROUP * HEAD_DIM
    return pl.pallas_call(
        _attn_a_body, name="attn_gqa",
        grid=(A_KV_HEADS, s // A_Q_TOKENS),
        in_specs=[pl.BlockSpec((rows, A_Q_TOKENS), lambda g, i: (g, i)),
                  pl.BlockSpec((None, s, HEAD_DIM), lambda g, i: (g, 0, 0)),
                  pl.BlockSpec((None, nk, HEAD_DIM + BF16_SUBLANES, KV_CHUNK), lambda g, i: (g, 0, 0, 0))],
        out_specs=pl.BlockSpec((rows, A_Q_TOKENS), lambda g, i: (g, i)),
        out_shape=jax.ShapeDtypeStruct((A_Q_COLS, s), _BF16),
        scratch_shapes=_attn_scratch(A_GROUP * A_Q_TOKENS),
        compiler_params=_params(2),
    )(qta, ka, vta)


def _attn_b_call(qtb, kb, vtb, slopes, lq1, lk1, lq2, lk2, subln, lambda_init):
    s = qtb.shape[1]
    nk = s // KV_CHUNK
    per_head = lambda width: pl.BlockSpec((None, 1, width), lambda h, i: (h, 0, 0))
    return pl.pallas_call(
        functools.partial(_attn_b_body, lambda_init=lambda_init), name="attn_diff",
        grid=(B_HEADS, s // B_Q_TOKENS),
        in_specs=[pl.BlockSpec((2 * HEAD_DIM, B_Q_TOKENS), lambda h, i: (h, i)),
                  pl.BlockSpec((s, 2 * HEAD_DIM), lambda h, i: (0, h)),
                  pl.BlockSpec((None, nk, B_VDIM + BF16_SUBLANES, KV_CHUNK), lambda h, i: (h, 0, 0, 0)),
                  per_head(1), per_head(HEAD_DIM), per_head(HEAD_DIM), per_head(HEAD_DIM), per_head(HEAD_DIM),
                  pl.BlockSpec((B_VDIM, 1), lambda h, i: (0, 0))],
        out_specs=pl.BlockSpec((B_VDIM, B_Q_TOKENS), lambda h, i: (h, i)),
        out_shape=jax.ShapeDtypeStruct((B_V_COLS, s), _BF16),
        scratch_shapes=_attn_scratch(2 * B_Q_TOKENS),
        compiler_params=_params(2),
    )(qtb, kb, vtb, slopes, lq1, lk1, lq2, lk2, subln)


def _out_call(x2d, at, bt, w_out, gain, wg, wu, wd):
    s = x2d.shape[0]
    row = pl.BlockSpec((ROW_TILE, D_MODEL), lambda i: (i, 0))
    col = pl.BlockSpec((at.shape[0], ROW_TILE), lambda i: (0, i))
    return pl.pallas_call(
        _out_body, name="out_proj_ffn",
        grid=(s // ROW_TILE,),
        in_specs=[row, col, col, _resident(w_out.shape), _resident((1, D_MODEL)),
                  _resident(wg.shape), _resident(wu.shape), _resident(wd.shape)],
        out_specs=row,
        out_shape=jax.ShapeDtypeStruct((s, D_MODEL), _F32),
        compiler_params=_params(1),
    )(x2d, at, bt, w_out, gain, wg, wu, wd)


def _rope_tables(s):
    rows = s // GRID_W
    row = jnp.broadcast_to(jnp.arange(rows, dtype=_F32)[:, None], (rows, GRID_W)).reshape(-1)
    colp = jnp.broadcast_to(jnp.arange(GRID_W, dtype=_F32)[None, :], (rows, GRID_W)).reshape(-1)
    inv_freq = ROPE_THETA ** (-jnp.arange(0, AX_DIM, 2, dtype=_F32) / AX_DIM)
    ang_r = row[:, None] * inv_freq[None, :]
    ang_c = colp[:, None] * inv_freq[None, :]
    cos = jnp.concatenate([jnp.cos(ang_r)] * 2 + [jnp.cos(ang_c)] * 2, axis=-1)
    sin = jnp.concatenate([-jnp.sin(ang_r), jnp.sin(ang_r), -jnp.sin(ang_c), jnp.sin(ang_c)], axis=-1)
    return jnp.tile(cos, (1, LANES // HEAD_DIM)), jnp.tile(sin, (1, LANES // HEAD_DIM))


def kernel(x, ffn1_norm, ffn1_w_gate, ffn1_w_up, ffn1_w_down, attn_norm, w_in, a_q_norm, a_k_norm, b_q_norm, b_k_norm, lambda_q1, lambda_k1, lambda_q2, lambda_k2, b_subln, w_out, ffn2_norm, ffn2_w_gate, ffn2_w_up, ffn2_w_down):
    bsz, s, _ = x.shape
    depth = w_in.shape[0]
    assert s % ROW_TILE == 0 and s % KV_CHUNK == 0 and s % A_Q_TOKENS == 0 and s % B_Q_TOKENS == 0
    cos, sin = _rope_tables(s)
    lane_chunk = jnp.arange(LANES) // HEAD_DIM
    seg = ((lane_chunk[:, None] == lane_chunk[None, :]).astype(_F32) / HEAD_DIM).astype(_BF16)
    slopes = (2.0 ** (-8.0 * (jnp.arange(B_HEADS, dtype=_F32) + 1.0) / B_HEADS)).reshape(B_HEADS, 1, 1)
    lane_gain = lambda g: jnp.tile(g, LANES // HEAD_DIM).reshape(1, LANES)
    bf = lambda w: w.astype(_BF16)

    outs = []
    for b in range(bsz):
        xb = x[b]
        for l in range(depth):
            lambda_init = 0.8 - 0.6 * math.exp(-0.3 * l)
            xb = _ffn_call(xb, ffn1_norm[l].reshape(1, D_MODEL), bf(ffn1_w_gate[l]), bf(ffn1_w_up[l]), bf(ffn1_w_down[l]))
            qta, ka, vta, qtb, kb, vtb = _prep_call(
                xb, attn_norm[l].reshape(1, D_MODEL), bf(w_in[l]), cos, sin,
                lane_gain(a_q_norm[l]), lane_gain(a_k_norm[l]), lane_gain(b_q_norm[l]), lane_gain(b_k_norm[l]), seg)
            at = _attn_a_call(qta, ka, vta)
            per_head = lambda p: p[l].reshape(B_HEADS, 1, HEAD_DIM)
            bt = _attn_b_call(qtb, kb, vtb, slopes, per_head(lambda_q1), per_head(lambda_k1), per_head(lambda_q2),
                              per_head(lambda_k2), b_subln[l].reshape(B_VDIM, 1), lambda_init)
            xb = _out_call(xb, at, bt, bf(w_out[l]), ffn2_norm[l].reshape(1, D_MODEL),
                           bf(ffn2_w_gate[l]), bf(ffn2_w_up[l]), bf(ffn2_w_down[l]))
        outs.append(xb)
    return jnp.stack(outs, axis=0)
```

```python
import functools
import math

import jax
import jax.numpy as jnp
from jax import lax
from jax.experimental import pallas as pl
from jax.experimental.pallas import tpu as pltpu

D_MODEL = 1024
HEAD_DIM = 64
GRID_W = 64
EPS = 1e-6
ROPE_THETA = 10000.0
AX_DIM = HEAD_DIM // 2
A_HEADS = 8
A_KV_HEADS = 2
A_GROUP = A_HEADS // A_KV_HEADS
B_HEADS = 4
B_VDIM = 2 * HEAD_DIM
A_Q_COLS = A_HEADS * HEAD_DIM
A_KV_COLS = A_KV_HEADS * HEAD_DIM
B_QK_COLS = B_HEADS * 2 * HEAD_DIM
B_V_COLS = B_HEADS * B_VDIM
IN_COLS = A_Q_COLS + 2 * A_KV_COLS + 2 * B_QK_COLS + B_V_COLS
D_FF = 2816

LANES = 128
BF16_SUBLANES = 16
ROW_TILE = 512
KV_CHUNK = 512
A_Q_TOKENS = 128
B_Q_TOKENS = 256
MAX_UNROLL = 8
FF_CHUNK = 1408
VMEM_LIMIT = 56 * 1024 * 1024
NEG_BIG = -1e30

KA_LANES = 2 * HEAD_DIM
KB_LANES = 4 * HEAD_DIM
POS_SPLIT = 128
BOUND_MARGIN = 1.0 + 2.0 ** -6
SHIFT_PATH_MAX_BOUND = 40.0
DEAD_SHIFT = 256.0

STAT_AQ, STAT_AK, STAT_BQ, STAT_BK, STAT_ROWS = 0, 4, 5, 9, 16

_F32 = jnp.float32
_BF16 = jnp.bfloat16


def _dot(a, b):
    return jnp.dot(a, b, preferred_element_type=_F32)


def _rms_rows(x, gain):
    return x * lax.rsqrt(jnp.mean(x * x, axis=-1, keepdims=True) + EPS) * gain


def _swiglu_half_step(x, gain, wg_ref, wu_ref, wd_ref):
    h = _rms_rows(x, gain).astype(_BF16)
    acc = jnp.zeros(x.shape, _F32)
    for c in range(D_FF // FF_CHUNK):
        cols = slice(c * FF_CHUNK, (c + 1) * FF_CHUNK)
        g = _dot(h, wg_ref[:, cols])
        u = _dot(h, wu_ref[:, cols])
        a = (g / (1.0 + jnp.exp(-g)) * u).astype(_BF16)
        acc = acc + _dot(a, wd_ref[cols, :])
    return x + 0.5 * acc


def _ffn_body(x_ref, gain_ref, wg_ref, wu_ref, wd_ref, o_ref):
    o_ref[...] = _swiglu_half_step(x_ref[...], gain_ref[...], wg_ref, wu_ref, wd_ref)


def _chunk_sums(x, seg):
    hi = x.astype(_BF16)
    lo = (x - hi.astype(_F32)).astype(_BF16)
    return _dot(hi, seg) + _dot(lo, seg)


def _prep_body(x_ref, gain_ref, w_in_ref, cos_ref, sin_ref, gqa_ref, gka_ref, gqb_ref, gkb_ref, seg_ref,
               qta_ref, ka_ref, vta_ref, qtb_ref, kb_ref, vtb_ref, stats_ref):
    rows = x_ref.shape[0]
    h = _rms_rows(x_ref[...], gain_ref[...]).astype(_BF16)
    proj = _dot(h, w_in_ref[...])
    seg = seg_ref[...]
    scale = 1.0 / math.sqrt(HEAD_DIM)

    @pl.when(pl.program_id(0) == 0)
    def _():
        stats_ref[...] = jnp.zeros(stats_ref.shape, _F32)

    def block(col):
        return proj[:, col:col + LANES]

    def qk_norm(blk, gain):
        return blk * lax.rsqrt(_chunk_sums(blk * blk, seg) + EPS) * gain

    def note_norm(stat_row, v16):
        f = v16.astype(_F32)
        sq = _chunk_sums(f * f, seg) * HEAD_DIM
        tile_max = jnp.max(sq, axis=0, keepdims=True)
        stats_ref[stat_row:stat_row + 1, :] = jnp.maximum(stats_ref[stat_row:stat_row + 1, :], tile_max)

    lane = lax.broadcasted_iota(jnp.int32, (rows, LANES), 1)
    first_half = (lane % AX_DIM) < (AX_DIM // 2)
    cos = cos_ref[...]
    sin = sin_ref[...]

    def rope(blk):
        partner = jnp.where(first_half, pltpu.roll(blk, LANES - AX_DIM // 2, 1), pltpu.roll(blk, AX_DIM // 2, 1))
        return blk * cos + partner * sin

    for b in range(A_Q_COLS // LANES):
        q = (rope(qk_norm(block(b * LANES), gqa_ref[...])) * scale).astype(_BF16)
        note_norm(STAT_AQ + b, q)
        qta_ref[b * LANES:(b + 1) * LANES, :] = q.astype(_F32).T.astype(_BF16)
    k = rope(qk_norm(block(A_Q_COLS), gka_ref[...])).astype(_BF16)
    note_norm(STAT_AK, k)
    vt = block(A_Q_COLS + A_KV_COLS).T.astype(_BF16)
    ones_rows = jnp.ones((BF16_SUBLANES, rows), _BF16)
    one_lane = (lax.broadcasted_iota(jnp.int32, (rows, HEAD_DIM), 1) == 0).astype(_BF16)
    for g in range(A_KV_HEADS):
        ka_ref[g] = jnp.concatenate([k[:, g * HEAD_DIM:(g + 1) * HEAD_DIM], one_lane], axis=1)
        vta_ref[g, 0, 0:HEAD_DIM, :] = vt[g * HEAD_DIM:(g + 1) * HEAD_DIM, :]
        vta_ref[g, 0, HEAD_DIM:HEAD_DIM + BF16_SUBLANES, :] = ones_rows
    pos = pl.program_id(0) * rows + lax.broadcasted_iota(jnp.int32, (rows, LANES), 0)
    pos_feat = jnp.where(lane == 0, (pos // POS_SPLIT * POS_SPLIT).astype(_F32),
                         jnp.where(lane == 1, (pos % POS_SPLIT).astype(_F32),
                                   jnp.where(lane < 5, 1.0, 0.0))).astype(_BF16)
    b_q0 = A_Q_COLS + 2 * A_KV_COLS
    b_k0 = b_q0 + B_QK_COLS
    b_v0 = b_k0 + B_QK_COLS
    for hd in range(B_HEADS):
        q = (qk_norm(block(b_q0 + hd * LANES), gqb_ref[...]) * scale).astype(_BF16)
        note_norm(STAT_BQ + hd, q)
        qtb_ref[hd * LANES:(hd + 1) * LANES, :] = q.astype(_F32).T.astype(_BF16)
        kb = qk_norm(block(b_k0 + hd * LANES), gkb_ref[...]).astype(_BF16)
        note_norm(STAT_BK + hd, kb)
        kb_ref[:, hd * KB_LANES:hd * KB_LANES + LANES] = kb
        kb_ref[:, hd * KB_LANES + LANES:(hd + 1) * KB_LANES] = pos_feat
        vtb_ref[hd, 0, 0:B_VDIM, :] = block(b_v0 + hd * LANES).T.astype(_BF16)
        vtb_ref[hd, 0, B_VDIM:B_VDIM + BF16_SUBLANES, :] = ones_rows


def _online_columns(w, k_ref, vt_ref, s_scr, p_scr, bias_fn):
    n = w.shape[1]
    rows = vt_ref.shape[1]
    nk = vt_ref.shape[0]
    assert nk % 2 == 0

    def scores(j, slot):
        k = k_ref[pl.ds(pl.multiple_of(j * KV_CHUNK, KV_CHUNK), KV_CHUNK), :]
        s = _dot(k, w)
        if bias_fn is not None:
            s = s - bias_fn(j)
        s_scr[slot] = s
        return jnp.max(s, axis=0, keepdims=True)

    def weights(slot, m, cmax):
        m_new = jnp.maximum(m, cmax)
        p_scr[slot] = jnp.exp(s_scr[slot] - m_new).astype(_BF16)
        return m_new, jnp.exp(m - m_new)

    def values(j, slot, acc, alpha):
        return alpha * acc + _dot(vt_ref[j], p_scr[slot])

    m = jnp.full((1, n), NEG_BIG, _F32)
    acc = jnp.zeros((rows, n), _F32)
    cmax0 = scores(0, 0)
    cmax1 = scores(1, 1)
    m, alpha = weights(0, m, cmax0)

    def body(i, carry):
        m, alpha, cmax1, acc = carry
        j = 2 * i + 1
        cmax0 = scores(j + 1, 0)
        acc = values(j - 1, 0, acc, alpha)
        m, alpha = weights(1, m, cmax1)
        cmax1 = scores(j + 2, 1)
        acc = values(j, 1, acc, alpha)
        m, alpha = weights(0, m, cmax0)
        return m, alpha, cmax1, acc

    m, alpha, cmax1, acc = lax.fori_loop(0, nk // 2 - 1, body, (m, alpha, cmax1, acc))
    acc = values(nk - 2, 0, acc, alpha)
    m, alpha = weights(1, m, cmax1)
    return values(nk - 1, 1, acc, alpha)


def _shifted_columns(w_fn, k_ref, vt_ref, acc):
    nk = vt_ref.shape[0]
    unroll = max(u for u in range(1, MAX_UNROLL + 1) if nk % u == 0)

    def chunk(j, acc):
        k = k_ref[pl.ds(pl.multiple_of(j * KV_CHUNK, KV_CHUNK), KV_CHUNK), :]
        p = jnp.exp(_dot(k, w_fn(j))).astype(_BF16)
        return acc + _dot(vt_ref[j], p)

    def body(i, acc):
        for c in range(unroll):
            acc = chunk(i * unroll + c, acc)
        return acc

    return lax.fori_loop(0, nk // unroll, body, acc)


def _column_bound(q16, k_sq):
    qf = q16.astype(_F32)
    return jnp.sqrt(jnp.sum(qf * qf, axis=0, keepdims=True) * k_sq) * BOUND_MARGIN


def _feature_rows(rows_list, n):
    row = lax.broadcasted_iota(jnp.int32, (BF16_SUBLANES, n), 0)
    tile = jnp.zeros((BF16_SUBLANES, n), _F32)
    for r, v in enumerate(rows_list):
        tile = jnp.where(row == r, v, tile)
    return tile.astype(_BF16)


def _lane_chunk_max(stat_row, chunk):
    lane = lax.broadcasted_iota(jnp.int32, stat_row.shape, 1)
    return jnp.max(jnp.where(lane // HEAD_DIM == chunk, stat_row, 0.0), axis=1, keepdims=True)


def _a_queries(qt_ref):
    return jnp.concatenate([qt_ref[r * HEAD_DIM:(r + 1) * HEAD_DIM, :] for r in range(A_GROUP)], axis=1)


def _a_finish(acc, o_ref):
    tq = o_ref.shape[1]
    out = acc[0:HEAD_DIM, :] / acc[HEAD_DIM:HEAD_DIM + 1, :]
    for r in range(A_GROUP):
        o_ref[r * HEAD_DIM:(r + 1) * HEAD_DIM, :] = out[:, r * tq:(r + 1) * tq].astype(o_ref.dtype)


def _attn_a_shift_body(qt_ref, k_ref, vt_ref, stats_ref, o_ref):
    wq = _a_queries(qt_ref)
    n = wq.shape[1]
    k_sq = _lane_chunk_max(stats_ref[STAT_AK:STAT_AK + 1, :], pl.program_id(0))
    feat = _feature_rows([-_column_bound(wq, k_sq)], n)
    w = jnp.concatenate([wq, feat, jnp.zeros((KA_LANES - HEAD_DIM - BF16_SUBLANES, n), _BF16)], axis=0)
    acc = _shifted_columns(lambda j: w, k_ref, vt_ref, jnp.zeros((vt_ref.shape[1], n), _F32))
    _a_finish(acc, o_ref)


def _attn_a_online_body(qt_ref, k_ref, vt_ref, o_ref, s_scr, p_scr):
    wq = _a_queries(qt_ref)
    w = jnp.concatenate([wq, jnp.zeros((KA_LANES - HEAD_DIM, wq.shape[1]), _BF16)], axis=0)
    _a_finish(_online_columns(w, k_ref, vt_ref, s_scr, p_scr, None), o_ref)


def _b_queries(qt_ref):
    tq = qt_ref.shape[1]
    zeros = jnp.zeros((HEAD_DIM, tq), _BF16)
    return jnp.concatenate([jnp.concatenate([qt_ref[0:HEAD_DIM, :], zeros], axis=1),
                            jnp.concatenate([zeros, qt_ref[HEAD_DIM:2 * HEAD_DIM, :]], axis=1)], axis=0)


def _b_finish(acc, lq1_ref, lk1_ref, lq2_ref, lk2_ref, subln_ref, o_ref, lambda_init):
    tq = o_ref.shape[1]
    lam = (jnp.exp(jnp.sum(lq1_ref[...] * lk1_ref[...], axis=-1, keepdims=True))
           - jnp.exp(jnp.sum(lq2_ref[...] * lk2_ref[...], axis=-1, keepdims=True)) + lambda_init)
    o0 = acc[0:B_VDIM, 0:tq] / acc[B_VDIM:B_VDIM + 1, 0:tq]
    o1 = acc[0:B_VDIM, tq:2 * tq] / acc[B_VDIM:B_VDIM + 1, tq:2 * tq]
    o = o0 - lam * o1
    o = o * lax.rsqrt(jnp.mean(o * o, axis=0, keepdims=True) + EPS) * subln_ref[...] * (1.0 - lambda_init)
    o_ref[...] = o.astype(o_ref.dtype)


def _attn_b_shift_body(qt_ref, k_ref, vt_ref, stats_ref, slope_ref, lq1_ref, lk1_ref, lq2_ref, lk2_ref,
                       subln_ref, o_ref, w_scr, *, lambda_init):
    tq = qt_ref.shape[1]
    n = 2 * tq
    head = pl.program_id(0)
    wq = _b_queries(qt_ref)
    k_stat = stats_ref[pl.ds(STAT_BK + head, 1), :]
    bound = jnp.concatenate([_column_bound(qt_ref[0:HEAD_DIM, :], _lane_chunk_max(k_stat, 0)),
                             _column_bound(qt_ref[HEAD_DIM:2 * HEAD_DIM, :], _lane_chunk_max(k_stat, 1))], axis=1)
    slope = jnp.broadcast_to(slope_ref[...], (1, n))
    q_pos = pl.program_id(1) * tq + lax.broadcasted_iota(jnp.int32, (1, tq), 1)
    q_hi = (q_pos // POS_SPLIT * POS_SPLIT).astype(_F32)
    q_lo = (q_pos % POS_SPLIT).astype(_F32)
    sq_hi = slope * jnp.concatenate([q_hi, q_hi], axis=1)
    sq_lo = slope * jnp.concatenate([q_lo, q_lo], axis=1)
    pad = jnp.zeros((KB_LANES - 2 * HEAD_DIM - BF16_SUBLANES, n), _BF16)
    zero = jnp.zeros((1, n), _F32)
    w_scr[0] = jnp.concatenate([wq, _feature_rows([slope, slope, -sq_hi, -sq_lo, -bound], n), pad], axis=0)
    w_scr[1] = jnp.concatenate([wq, _feature_rows([-slope, -slope, sq_hi, sq_lo, -bound], n), pad], axis=0)
    w_scr[2] = jnp.concatenate([wq, _feature_rows([zero, zero, zero, zero, -bound - DEAD_SHIFT], n), pad], axis=0)

    diag = (pl.program_id(1) * tq) // KV_CHUNK
    k_diag = k_ref[pl.ds(pl.multiple_of(diag * KV_CHUNK, KV_CHUNK), KV_CHUNK), :]
    t = jnp.minimum(_dot(k_diag, w_scr[0]), _dot(k_diag, w_scr[1]))
    acc = _dot(vt_ref[diag], jnp.exp(t).astype(_BF16))

    def w_fn(j):
        return w_scr[jnp.where(j < diag, 0, jnp.where(j > diag, 1, 2))]

    acc = _shifted_columns(w_fn, k_ref, vt_ref, acc)
    _b_finish(acc, lq1_ref, lk1_ref, lq2_ref, lk2_ref, subln_ref, o_ref, lambda_init)


def _attn_b_online_body(qt_ref, k_ref, vt_ref, slope_ref, lq1_ref, lk1_ref, lq2_ref, lk2_ref, subln_ref, o_ref,
                        s_scr, p_scr, *, lambda_init):
    tq = qt_ref.shape[1]
    wq = _b_queries(qt_ref)
    w = jnp.concatenate([wq, jnp.zeros((KB_LANES - 2 * HEAD_DIM, 2 * tq), _BF16)], axis=0)
    slope = slope_ref[...]
    q_pos = pl.program_id(1) * tq + lax.broadcasted_iota(jnp.int32, (KV_CHUNK, tq), 1)
    k_off = lax.broadcasted_iota(jnp.int32, (KV_CHUNK, tq), 0)

    def bias_fn(j):
        dist = jnp.abs(q_pos - (k_off + j * KV_CHUNK)).astype(_F32)
        bias = slope * dist
        return jnp.concatenate([bias, bias], axis=1)

    acc = _online_columns(w, k_ref, vt_ref, s_scr, p_scr, bias_fn)
    _b_finish(acc, lq1_ref, lk1_ref, lq2_ref, lk2_ref, subln_ref, o_ref, lambda_init)


def _out_body(x_ref, at_ref, bt_ref, w_out_ref, gain_ref, wg_ref, wu_ref, wd_ref, o_ref):
    tn = (((0,), (0,)), ((), ()))
    half = at_ref.shape[0]
    y = (lax.dot_general(at_ref[...], w_out_ref[0:half, :], tn, preferred_element_type=_F32)
         + lax.dot_general(bt_ref[...], w_out_ref[half:2 * half, :], tn, preferred_element_type=_F32))
    x = x_ref[...] + y
    o_ref[...] = _swiglu_half_step(x, gain_ref[...], wg_ref, wu_ref, wd_ref)


def _resident(shape):
    return pl.BlockSpec(shape, lambda *_: (0,) * len(shape), pipeline_mode=pl.Buffered(1))


def _online_scratch(n):
    return [pltpu.VMEM((2, KV_CHUNK, n), _F32), pltpu.VMEM((2, KV_CHUNK, n), _BF16)]


def _params(n_axes):
    return pltpu.CompilerParams(dimension_semantics=("arbitrary",) * n_axes, vmem_limit_bytes=VMEM_LIMIT)


def _ffn_call(x2d, gain, wg, wu, wd):
    s = x2d.shape[0]
    row = pl.BlockSpec((ROW_TILE, D_MODEL), lambda i: (i, 0))
    return pl.pallas_call(
        _ffn_body, name="ffn_half_step",
        grid=(s // ROW_TILE,),
        in_specs=[row, _resident((1, D_MODEL)), _resident(wg.shape), _resident(wu.shape), _resident(wd.shape)],
        out_specs=row,
        out_shape=jax.ShapeDtypeStruct((s, D_MODEL), _F32),
        compiler_params=_params(1),
    )(x2d, gain, wg, wu, wd)


def _prep_call(x2d, gain, w_in, cos, sin, gqa, gka, gqb, gkb, seg):
    s = x2d.shape[0]
    nk = s // KV_CHUNK
    assert ROW_TILE == KV_CHUNK
    row = lambda width: pl.BlockSpec((ROW_TILE, width), lambda i: (i, 0))
    col = lambda height: pl.BlockSpec((height, ROW_TILE), lambda i: (0, i))
    vec = _resident((1, LANES))
    out_shape = (
        jax.ShapeDtypeStruct((A_Q_COLS, s), _BF16),
        jax.ShapeDtypeStruct((A_KV_HEADS, s, KA_LANES), _BF16),
        jax.ShapeDtypeStruct((A_KV_HEADS, nk, HEAD_DIM + BF16_SUBLANES, KV_CHUNK), _BF16),
        jax.ShapeDtypeStruct((B_QK_COLS, s), _BF16),
        jax.ShapeDtypeStruct((s, B_HEADS * KB_LANES), _BF16),
        jax.ShapeDtypeStruct((B_HEADS, nk, B_VDIM + BF16_SUBLANES, KV_CHUNK), _BF16),
        jax.ShapeDtypeStruct((STAT_ROWS, LANES), _F32),
    )
    out_specs = (
        col(A_Q_COLS),
        pl.BlockSpec((A_KV_HEADS, ROW_TILE, KA_LANES), lambda i: (0, i, 0)),
        pl.BlockSpec((A_KV_HEADS, 1, HEAD_DIM + BF16_SUBLANES, KV_CHUNK), lambda i: (0, i, 0, 0)),
        col(B_QK_COLS),
        row(B_HEADS * KB_LANES),
        pl.BlockSpec((B_HEADS, 1, B_VDIM + BF16_SUBLANES, KV_CHUNK), lambda i: (0, i, 0, 0)),
        pl.BlockSpec((STAT_ROWS, LANES), lambda i: (0, 0)),
    )
    return pl.pallas_call(
        _prep_body, name="in_proj_prep",
        grid=(s // ROW_TILE,),
        in_specs=[row(D_MODEL), _resident((1, D_MODEL)), _resident(w_in.shape), row(LANES), row(LANES),
                  vec, vec, vec, vec, _resident((LANES, LANES))],
        out_specs=out_specs,
        out_shape=out_shape,
        compiler_params=_params(1),
    )(x2d, gain, w_in, cos, sin, gqa, gka, gqb, gkb, seg)


def _attn_a_call(qta, ka, vta, stats, shift_path):
    s = qta.shape[1]
    nk = s // KV_CHUNK
    rows = A_GROUP * HEAD_DIM
    in_specs = [pl.BlockSpec((rows, A_Q_TOKENS), lambda g, i: (g, i)),
                pl.BlockSpec((None, s, KA_LANES), lambda g, i: (g, 0, 0)),
                pl.BlockSpec((None, nk, HEAD_DIM + BF16_SUBLANES, KV_CHUNK), lambda g, i: (g, 0, 0, 0))]
    common = dict(
        grid=(A_KV_HEADS, s // A_Q_TOKENS),
        out_specs=pl.BlockSpec((rows, A_Q_TOKENS), lambda g, i: (g, i)),
        out_shape=jax.ShapeDtypeStruct((A_Q_COLS, s), _BF16),
        compiler_params=_params(2))
    if shift_path:
        return pl.pallas_call(_attn_a_shift_body, name="attn_gqa_shift",
                              in_specs=in_specs + [pl.BlockSpec((STAT_ROWS, LANES), lambda g, i: (0, 0))],
                              **common)(qta, ka, vta, stats)
    return pl.pallas_call(_attn_a_online_body, name="attn_gqa_online", in_specs=in_specs,
                          scratch_shapes=_online_scratch(A_GROUP * A_Q_TOKENS), **common)(qta, ka, vta)


def _attn_b_call(qtb, kb, vtb, stats, slopes, lq1, lk1, lq2, lk2, subln, lambda_init, shift_path):
    s = qtb.shape[1]
    nk = s // KV_CHUNK
    n = 2 * B_Q_TOKENS
    per_head = lambda width: pl.BlockSpec((None, 1, width), lambda h, i: (h, 0, 0))
    qkv_specs = [pl.BlockSpec((2 * HEAD_DIM, B_Q_TOKENS), lambda h, i: (h, i)),
                 pl.BlockSpec((s, KB_LANES), lambda h, i: (0, h)),
                 pl.BlockSpec((None, nk, B_VDIM + BF16_SUBLANES, KV_CHUNK), lambda h, i: (h, 0, 0, 0))]
    param_specs = [per_head(1), per_head(HEAD_DIM), per_head(HEAD_DIM), per_head(HEAD_DIM), per_head(HEAD_DIM),
                   pl.BlockSpec((B_VDIM, 1), lambda h, i: (0, 0))]
    common = dict(
        grid=(B_HEADS, s // B_Q_TOKENS),
        out_specs=pl.BlockSpec((B_VDIM, B_Q_TOKENS), lambda h, i: (h, i)),
        out_shape=jax.ShapeDtypeStruct((B_V_COLS, s), _BF16),
        compiler_params=_params(2))
    params = (slopes, lq1, lk1, lq2, lk2, subln)
    if shift_path:
        return pl.pallas_call(
            functools.partial(_attn_b_shift_body, lambda_init=lambda_init), name="attn_diff_shift",
            in_specs=qkv_specs + [pl.BlockSpec((STAT_ROWS, LANES), lambda h, i: (0, 0))] + param_specs,
            scratch_shapes=[pltpu.VMEM((3, KB_LANES, n), _BF16)], **common)(qtb, kb, vtb, stats, *params)
    return pl.pallas_call(
        functools.partial(_attn_b_online_body, lambda_init=lambda_init), name="attn_diff_online",
        in_specs=qkv_specs + param_specs, scratch_shapes=_online_scratch(n), **common)(qtb, kb, vtb, *params)


def _out_call(x2d, at, bt, w_out, gain, wg, wu, wd):
    s = x2d.shape[0]
    row = pl.BlockSpec((ROW_TILE, D_MODEL), lambda i: (i, 0))
    col = pl.BlockSpec((at.shape[0], ROW_TILE), lambda i: (0, i))
    return pl.pallas_call(
        _out_body, name="out_proj_ffn",
        grid=(s // ROW_TILE,),
        in_specs=[row, col, col, _resident(w_out.shape), _resident((1, D_MODEL)),
                  _resident(wg.shape), _resident(wu.shape), _resident(wd.shape)],
        out_specs=row,
        out_shape=jax.ShapeDtypeStruct((s, D_MODEL), _F32),
        compiler_params=_params(1),
    )(x2d, at, bt, w_out, gain, wg, wu, wd)


def _rope_tables(s):
    rows = s // GRID_W
    row = jnp.broadcast_to(jnp.arange(rows, dtype=_F32)[:, None], (rows, GRID_W)).reshape(-1)
    colp = jnp.broadcast_to(jnp.arange(GRID_W, dtype=_F32)[None, :], (rows, GRID_W)).reshape(-1)
    inv_freq = ROPE_THETA ** (-jnp.arange(0, AX_DIM, 2, dtype=_F32) / AX_DIM)
    ang_r = row[:, None] * inv_freq[None, :]
    ang_c = colp[:, None] * inv_freq[None, :]
    cos = jnp.concatenate([jnp.cos(ang_r)] * 2 + [jnp.cos(ang_c)] * 2, axis=-1)
    sin = jnp.concatenate([-jnp.sin(ang_r), jnp.sin(ang_r), -jnp.sin(ang_c), jnp.sin(ang_c)], axis=-1)
    return jnp.tile(cos, (1, LANES // HEAD_DIM)), jnp.tile(sin, (1, LANES // HEAD_DIM))


def _max_score_bound(stats):
    chunks = stats[:, ::HEAD_DIM]
    a_q = chunks[STAT_AQ:STAT_AQ + A_HEADS // 2].reshape(A_KV_HEADS, A_GROUP)
    a = jnp.max(a_q * chunks[STAT_AK][:, None])
    b = jnp.max(chunks[STAT_BQ:STAT_BQ + B_HEADS] * chunks[STAT_BK:STAT_BK + B_HEADS])
    return jnp.sqrt(jnp.maximum(a, b)) * BOUND_MARGIN


def kernel(x, ffn1_norm, ffn1_w_gate, ffn1_w_up, ffn1_w_down, attn_norm, w_in, a_q_norm, a_k_norm, b_q_norm, b_k_norm, lambda_q1, lambda_k1, lambda_q2, lambda_k2, b_subln, w_out, ffn2_norm, ffn2_w_gate, ffn2_w_up, ffn2_w_down):
    bsz, s, _ = x.shape
    depth = w_in.shape[0]
    assert s % ROW_TILE == 0 and s % KV_CHUNK == 0 and s % A_Q_TOKENS == 0 and s % B_Q_TOKENS == 0
    assert KV_CHUNK % B_Q_TOKENS == 0 and s < POS_SPLIT * 256
    cos, sin = _rope_tables(s)
    lane_chunk = jnp.arange(LANES) // HEAD_DIM
    seg = ((lane_chunk[:, None] == lane_chunk[None, :]).astype(_F32) / HEAD_DIM).astype(_BF16)
    slopes = (2.0 ** (-8.0 * (jnp.arange(B_HEADS, dtype=_F32) + 1.0) / B_HEADS)).reshape(B_HEADS, 1, 1)
    lane_gain = lambda g: jnp.tile(g, LANES // HEAD_DIM).reshape(1, LANES)
    bf = lambda w: w.astype(_BF16)

    outs = []
    for b in range(bsz):
        xb = x[b]
        for l in range(depth):
            lambda_init = 0.8 - 0.6 * math.exp(-0.3 * l)
            xb = _ffn_call(xb, ffn1_norm[l].reshape(1, D_MODEL), bf(ffn1_w_gate[l]), bf(ffn1_w_up[l]), bf(ffn1_w_down[l]))
            qta, ka, vta, qtb, kb, vtb, stats = _prep_call(
                xb, attn_norm[l].reshape(1, D_MODEL), bf(w_in[l]), cos, sin,
                lane_gain(a_q_norm[l]), lane_gain(a_k_norm[l]), lane_gain(b_q_norm[l]), lane_gain(b_k_norm[l]), seg)
            per_head = lambda p: p[l].reshape(B_HEADS, 1, HEAD_DIM)
            b_params = (slopes, per_head(lambda_q1), per_head(lambda_k1), per_head(lambda_q2), per_head(lambda_k2),
                        b_subln[l].reshape(B_VDIM, 1), lambda_init)

            def mixers(shift_path, qta=qta, ka=ka, vta=vta, qtb=qtb, kb=kb, vtb=vtb, stats=stats, b_params=b_params):
                return (_attn_a_call(qta, ka, vta, stats, shift_path),
                        _attn_b_call(qtb, kb, vtb, stats, *b_params, shift_path))

            at, bt = lax.cond(_max_score_bound(stats) <= SHIFT_PATH_MAX_BOUND,
                              lambda: mixers(True), lambda: mixers(False))
            xb = _out_call(xb, at, bt, bf(w_out[l]), ffn2_norm[l].reshape(1, D_MODEL),
                           bf(ffn2_w_gate[l]), bf(ffn2_w_up[l]), bf(ffn2_w_down[l]))
        outs.append(xb)
    return jnp.stack(outs, axis=0)
```

```python
import functools
import math

import jax
import jax.numpy as jnp
from jax import lax
from jax.experimental import pallas as pl
from jax.experimental.pallas import tpu as pltpu

D_MODEL = 1024
HEAD_DIM = 64
GRID_W = 64
EPS = 1e-6
ROPE_THETA = 10000.0
AX_DIM = HEAD_DIM // 2
A_HEADS = 8
A_KV_HEADS = 2
A_GROUP = A_HEADS // A_KV_HEADS
B_HEADS = 4
B_VDIM = 2 * HEAD_DIM
A_Q_COLS = A_HEADS * HEAD_DIM
A_KV_COLS = A_KV_HEADS * HEAD_DIM
B_QK_COLS = B_HEADS * 2 * HEAD_DIM
B_V_COLS = B_HEADS * B_VDIM
IN_COLS = A_Q_COLS + 2 * A_KV_COLS + 2 * B_QK_COLS + B_V_COLS
D_FF = 2816

LANES = 128
BF16_SUBLANES = 16
ROW_TILE = 512
KV_CHUNK = 512
A_Q_TOKENS = 128
B_Q_TOKENS = 256
FF_CHUNK = 1408
VMEM_LIMIT = 56 * 1024 * 1024
NEG_BIG = -1e30

KA_LANES = 2 * HEAD_DIM
KB_LANES = 4 * HEAD_DIM
POS_SPLIT = 128
BOUND_MARGIN = 1.0 + 2.0 ** -6
SHIFT_PATH_MAX_BOUND = 40.0
DEAD_SHIFT = 256.0

STAT_AQ, STAT_AK, STAT_BQ, STAT_BK, STAT_ROWS = 0, 4, 5, 9, 16

_F32 = jnp.float32
_BF16 = jnp.bfloat16


def _dot(a, b):
    return jnp.dot(a, b, preferred_element_type=_F32)


def _rms_rows(x, gain):
    return x * lax.rsqrt(jnp.mean(x * x, axis=-1, keepdims=True) + EPS) * gain


def _swiglu_half_step(x, gain, wg_ref, wu_ref, wd_ref):
    h = _rms_rows(x, gain).astype(_BF16)
    acc = jnp.zeros(x.shape, _F32)
    for c in range(D_FF // FF_CHUNK):
        cols = slice(c * FF_CHUNK, (c + 1) * FF_CHUNK)
        g = _dot(h, wg_ref[:, cols])
        u = _dot(h, wu_ref[:, cols])
        a = (g / (1.0 + jnp.exp(-g)) * u).astype(_BF16)
        acc = acc + _dot(a, wd_ref[cols, :])
    return x + 0.5 * acc


def _ffn_body(x_ref, gain_ref, wg_ref, wu_ref, wd_ref, o_ref):
    o_ref[...] = _swiglu_half_step(x_ref[...], gain_ref[...], wg_ref, wu_ref, wd_ref)


def _chunk_sums(x, seg):
    hi = x.astype(_BF16)
    lo = (x - hi.astype(_F32)).astype(_BF16)
    return _dot(hi, seg) + _dot(lo, seg)


def _prep_body(x_ref, gain_ref, w_in_ref, cos_ref, sin_ref, gqa_ref, gka_ref, gqb_ref, gkb_ref, seg_ref,
               qta_ref, ka_ref, vta_ref, qtb_ref, kb_ref, vtb_ref, stats_ref):
    rows = x_ref.shape[0]
    h = _rms_rows(x_ref[...], gain_ref[...]).astype(_BF16)
    proj = _dot(h, w_in_ref[...])
    seg = seg_ref[...]
    scale = 1.0 / math.sqrt(HEAD_DIM)

    @pl.when(pl.program_id(0) == 0)
    def _():
        stats_ref[...] = jnp.zeros(stats_ref.shape, _F32)

    def block(col):
        return proj[:, col:col + LANES]

    def qk_norm(blk, gain):
        return blk * lax.rsqrt(_chunk_sums(blk * blk, seg) + EPS) * gain

    def note_norm(stat_row, v16):
        f = v16.astype(_F32)
        sq = _chunk_sums(f * f, seg) * HEAD_DIM
        tile_max = jnp.max(sq, axis=0, keepdims=True)
        stats_ref[stat_row:stat_row + 1, :] = jnp.maximum(stats_ref[stat_row:stat_row + 1, :], tile_max)

    lane = lax.broadcasted_iota(jnp.int32, (rows, LANES), 1)
    first_half = (lane % AX_DIM) < (AX_DIM // 2)
    cos = cos_ref[...]
    sin = sin_ref[...]

    def rope(blk):
        partner = jnp.where(first_half, pltpu.roll(blk, LANES - AX_DIM // 2, 1), pltpu.roll(blk, AX_DIM // 2, 1))
        return blk * cos + partner * sin

    for b in range(A_Q_COLS // LANES):
        q = (rope(qk_norm(block(b * LANES), gqa_ref[...])) * scale).astype(_BF16)
        note_norm(STAT_AQ + b, q)
        qta_ref[b * LANES:(b + 1) * LANES, :] = q.astype(_F32).T.astype(_BF16)
    k = rope(qk_norm(block(A_Q_COLS), gka_ref[...])).astype(_BF16)
    note_norm(STAT_AK, k)
    vt = block(A_Q_COLS + A_KV_COLS).T.astype(_BF16)
    ones_rows = jnp.ones((BF16_SUBLANES, rows), _BF16)
    one_lane = (lax.broadcasted_iota(jnp.int32, (rows, HEAD_DIM), 1) == 0).astype(_BF16)
    for g in range(A_KV_HEADS):
        ka_ref[g] = jnp.concatenate([k[:, g * HEAD_DIM:(g + 1) * HEAD_DIM], one_lane], axis=1)
        vta_ref[g, 0, 0:HEAD_DIM, :] = vt[g * HEAD_DIM:(g + 1) * HEAD_DIM, :]
        vta_ref[g, 0, HEAD_DIM:HEAD_DIM + BF16_SUBLANES, :] = ones_rows
    pos = pl.program_id(0) * rows + lax.broadcasted_iota(jnp.int32, (rows, LANES), 0)
    pos_feat = jnp.where(lane == 0, (pos // POS_SPLIT * POS_SPLIT).astype(_F32),
                         jnp.where(lane == 1, (pos % POS_SPLIT).astype(_F32),
                                   jnp.where(lane < 5, 1.0, 0.0))).astype(_BF16)
    b_q0 = A_Q_COLS + 2 * A_KV_COLS
    b_k0 = b_q0 + B_QK_COLS
    b_v0 = b_k0 + B_QK_COLS
    for hd in range(B_HEADS):
        q = (qk_norm(block(b_q0 + hd * LANES), gqb_ref[...]) * scale).astype(_BF16)
        note_norm(STAT_BQ + hd, q)
        qtb_ref[hd * LANES:(hd + 1) * LANES, :] = q.astype(_F32).T.astype(_BF16)
        kb = qk_norm(block(b_k0 + hd * LANES), gkb_ref[...]).astype(_BF16)
        note_norm(STAT_BK + hd, kb)
        kb_ref[:, hd * KB_LANES:hd * KB_LANES + LANES] = kb
        kb_ref[:, hd * KB_LANES + LANES:(hd + 1) * KB_LANES] = pos_feat
        vtb_ref[hd, 0, 0:B_VDIM, :] = block(b_v0 + hd * LANES).T.astype(_BF16)
        vtb_ref[hd, 0, B_VDIM:B_VDIM + BF16_SUBLANES, :] = ones_rows


def _online_columns(w, k_ref, vt_ref, s_scr, p_scr, bias_fn):
    n = w.shape[1]
    rows = vt_ref.shape[1]
    nk = vt_ref.shape[0]
    assert nk % 2 == 0

    def scores(j, slot):
        k = k_ref[pl.ds(pl.multiple_of(j * KV_CHUNK, KV_CHUNK), KV_CHUNK), :]
        s = _dot(k, w)
        if bias_fn is not None:
            s = s - bias_fn(j)
        s_scr[slot] = s
        return jnp.max(s, axis=0, keepdims=True)

    def weights(slot, m, cmax):
        m_new = jnp.maximum(m, cmax)
        p_scr[slot] = jnp.exp(s_scr[slot] - m_new).astype(_BF16)
        return m_new, jnp.exp(m - m_new)

    def values(j, slot, acc, alpha):
        return alpha * acc + _dot(vt_ref[j], p_scr[slot])

    m = jnp.full((1, n), NEG_BIG, _F32)
    acc = jnp.zeros((rows, n), _F32)
    cmax0 = scores(0, 0)
    cmax1 = scores(1, 1)
    m, alpha = weights(0, m, cmax0)

    def body(i, carry):
        m, alpha, cmax1, acc = carry
        j = 2 * i + 1
        cmax0 = scores(j + 1, 0)
        acc = values(j - 1, 0, acc, alpha)
        m, alpha = weights(1, m, cmax1)
        cmax1 = scores(j + 2, 1)
        acc = values(j, 1, acc, alpha)
        m, alpha = weights(0, m, cmax0)
        return m, alpha, cmax1, acc

    m, alpha, cmax1, acc = lax.fori_loop(0, nk // 2 - 1, body, (m, alpha, cmax1, acc))
    acc = values(nk - 2, 0, acc, alpha)
    m, alpha = weights(1, m, cmax1)
    return values(nk - 1, 1, acc, alpha)


def _shifted_columns(w_fn, k_ref, vt_ref, s_scr, acc):
    nk = vt_ref.shape[0]

    def scores(j, slot):
        k = k_ref[pl.ds(j * KV_CHUNK, KV_CHUNK), :]
        s_scr[slot] = _dot(k, w_fn(j))

    scores(0, 0)
    for c in range(nk):
        if c + 1 < nk:
            scores(c + 1, (c + 1) % 2)
        p = jnp.exp(s_scr[c % 2]).astype(_BF16)
        acc = acc + _dot(vt_ref[c], p)
    return acc


def _column_bound(q16, k_sq):
    qf = q16.astype(_F32)
    return jnp.sqrt(jnp.sum(qf * qf, axis=0, keepdims=True) * k_sq) * BOUND_MARGIN


def _feature_rows(rows_list, n):
    row = lax.broadcasted_iota(jnp.int32, (BF16_SUBLANES, n), 0)
    tile = jnp.zeros((BF16_SUBLANES, n), _F32)
    for r, v in enumerate(rows_list):
        tile = jnp.where(row == r, v, tile)
    return tile.astype(_BF16)


def _lane_chunk_max(stat_row, chunk):
    lane = lax.broadcasted_iota(jnp.int32, stat_row.shape, 1)
    return jnp.max(jnp.where(lane // HEAD_DIM == chunk, stat_row, 0.0), axis=1, keepdims=True)


def _a_queries(qt_ref):
    return jnp.concatenate([qt_ref[r * HEAD_DIM:(r + 1) * HEAD_DIM, :] for r in range(A_GROUP)], axis=1)


def _a_finish(acc, o_ref):
    tq = o_ref.shape[1]
    out = acc[0:HEAD_DIM, :] / acc[HEAD_DIM:HEAD_DIM + 1, :]
    for r in range(A_GROUP):
        o_ref[r * HEAD_DIM:(r + 1) * HEAD_DIM, :] = out[:, r * tq:(r + 1) * tq].astype(o_ref.dtype)


def _attn_a_shift_body(qt_ref, k_ref, vt_ref, stats_ref, o_ref, s_scr):
    wq = _a_queries(qt_ref)
    n = wq.shape[1]
    k_sq = _lane_chunk_max(stats_ref[STAT_AK:STAT_AK + 1, :], pl.program_id(0))
    feat = _feature_rows([-_column_bound(wq, k_sq)], n)
    w = jnp.concatenate([wq, feat, jnp.zeros((KA_LANES - HEAD_DIM - BF16_SUBLANES, n), _BF16)], axis=0)
    acc = _shifted_columns(lambda j: w, k_ref, vt_ref, s_scr, jnp.zeros((vt_ref.shape[1], n), _F32))
    _a_finish(acc, o_ref)


def _attn_a_online_body(qt_ref, k_ref, vt_ref, o_ref, s_scr, p_scr):
    wq = _a_queries(qt_ref)
    w = jnp.concatenate([wq, jnp.zeros((KA_LANES - HEAD_DIM, wq.shape[1]), _BF16)], axis=0)
    _a_finish(_online_columns(w, k_ref, vt_ref, s_scr, p_scr, None), o_ref)


def _b_queries(qt_ref):
    tq = qt_ref.shape[1]
    zeros = jnp.zeros((HEAD_DIM, tq), _BF16)
    return jnp.concatenate([jnp.concatenate([qt_ref[0:HEAD_DIM, :], zeros], axis=1),
                            jnp.concatenate([zeros, qt_ref[HEAD_DIM:2 * HEAD_DIM, :]], axis=1)], axis=0)


def _b_finish(acc, lq1_ref, lk1_ref, lq2_ref, lk2_ref, subln_ref, o_ref, lambda_init):
    tq = o_ref.shape[1]
    lam = (jnp.exp(jnp.sum(lq1_ref[...] * lk1_ref[...], axis=-1, keepdims=True))
           - jnp.exp(jnp.sum(lq2_ref[...] * lk2_ref[...], axis=-1, keepdims=True)) + lambda_init)
    o0 = acc[0:B_VDIM, 0:tq] / acc[B_VDIM:B_VDIM + 1, 0:tq]
    o1 = acc[0:B_VDIM, tq:2 * tq] / acc[B_VDIM:B_VDIM + 1, tq:2 * tq]
    o = o0 - lam * o1
    o = o * lax.rsqrt(jnp.mean(o * o, axis=0, keepdims=True) + EPS) * subln_ref[...] * (1.0 - lambda_init)
    o_ref[...] = o.astype(o_ref.dtype)


def _attn_b_shift_body(qt_ref, k_ref, vt_ref, stats_ref, slope_ref, lq1_ref, lk1_ref, lq2_ref, lk2_ref,
                       subln_ref, o_ref, w_scr, s_scr, *, lambda_init):
    tq = qt_ref.shape[1]
    n = 2 * tq
    head = pl.program_id(0)
    wq = _b_queries(qt_ref)
    k_stat = stats_ref[pl.ds(STAT_BK + head, 1), :]
    bound = jnp.concatenate([_column_bound(qt_ref[0:HEAD_DIM, :], _lane_chunk_max(k_stat, 0)),
                             _column_bound(qt_ref[HEAD_DIM:2 * HEAD_DIM, :], _lane_chunk_max(k_stat, 1))], axis=1)
    slope = jnp.broadcast_to(slope_ref[...], (1, n))
    q_pos = pl.program_id(1) * tq + lax.broadcasted_iota(jnp.int32, (1, tq), 1)
    q_hi = (q_pos // POS_SPLIT * POS_SPLIT).astype(_F32)
    q_lo = (q_pos % POS_SPLIT).astype(_F32)
    sq_hi = slope * jnp.concatenate([q_hi, q_hi], axis=1)
    sq_lo = slope * jnp.concatenate([q_lo, q_lo], axis=1)
    pad = jnp.zeros((KB_LANES - 2 * HEAD_DIM - BF16_SUBLANES, n), _BF16)
    zero = jnp.zeros((1, n), _F32)
    w_scr[0] = jnp.concatenate([wq, _feature_rows([slope, slope, -sq_hi, -sq_lo, -bound], n), pad], axis=0)
    w_scr[1] = jnp.concatenate([wq, _feature_rows([-slope, -slope, sq_hi, sq_lo, -bound], n), pad], axis=0)
    w_scr[2] = jnp.concatenate([wq, _feature_rows([zero, zero, zero, zero, -bound - DEAD_SHIFT], n), pad], axis=0)

    diag = (pl.program_id(1) * tq) // KV_CHUNK
    k_diag = k_ref[pl.ds(pl.multiple_of(diag * KV_CHUNK, KV_CHUNK), KV_CHUNK), :]
    t = jnp.minimum(_dot(k_diag, w_scr[0]), _dot(k_diag, w_scr[1]))
    acc = _dot(vt_ref[diag], jnp.exp(t).astype(_BF16))

    def w_fn(j):
        return w_scr[jnp.where(j < diag, 0, jnp.where(j > diag, 1, 2))]

    acc = _shifted_columns(w_fn, k_ref, vt_ref, s_scr, acc)
    _b_finish(acc, lq1_ref, lk1_ref, lq2_ref, lk2_ref, subln_ref, o_ref, lambda_init)


def _attn_b_online_body(qt_ref, k_ref, vt_ref, slope_ref, lq1_ref, lk1_ref, lq2_ref, lk2_ref, subln_ref, o_ref,
                        s_scr, p_scr, *, lambda_init):
    tq = qt_ref.shape[1]
    wq = _b_queries(qt_ref)
    w = jnp.concatenate([wq, jnp.zeros((KB_LANES - 2 * HEAD_DIM, 2 * tq), _BF16)], axis=0)
    slope = slope_ref[...]
    q_pos = pl.program_id(1) * tq + lax.broadcasted_iota(jnp.int32, (KV_CHUNK, tq), 1)
    k_off = lax.broadcasted_iota(jnp.int32, (KV_CHUNK, tq), 0)

    def bias_fn(j):
        dist = jnp.abs(q_pos - (k_off + j * KV_CHUNK)).astype(_F32)
        bias = slope * dist
        return jnp.concatenate([bias, bias], axis=1)

    acc = _online_columns(w, k_ref, vt_ref, s_scr, p_scr, bias_fn)
    _b_finish(acc, lq1_ref, lk1_ref, lq2_ref, lk2_ref, subln_ref, o_ref, lambda_init)


def _out_body(x_ref, at_ref, bt_ref, w_out_ref, gain_ref, wg_ref, wu_ref, wd_ref, o_ref):
    tn = (((0,), (0,)), ((), ()))
    half = at_ref.shape[0]
    y = (lax.dot_general(at_ref[...], w_out_ref[0:half, :], tn, preferred_element_type=_F32)
         + lax.dot_general(bt_ref[...], w_out_ref[half:2 * half, :], tn, preferred_element_type=_F32))
    x = x_ref[...] + y
    o_ref[...] = _swiglu_half_step(x, gain_ref[...], wg_ref, wu_ref, wd_ref)


def _resident(shape):
    return pl.BlockSpec(shape, lambda *_: (0,) * len(shape), pipeline_mode=pl.Buffered(1))


def _score_scratch(n):
    return pltpu.VMEM((2, KV_CHUNK, n), _F32)


def _online_scratch(n):
    return [_score_scratch(n), pltpu.VMEM((2, KV_CHUNK, n), _BF16)]


def _params(n_axes, flags=None):
    return pltpu.CompilerParams(dimension_semantics=("arbitrary",) * n_axes, vmem_limit_bytes=VMEM_LIMIT,
                                flags=flags)


def _ffn_call(x2d, gain, wg, wu, wd):
    s = x2d.shape[0]
    row = pl.BlockSpec((ROW_TILE, D_MODEL), lambda i: (i, 0))
    return pl.pallas_call(
        _ffn_body, name="ffn_half_step",
        grid=(s // ROW_TILE,),
        in_specs=[row, _resident((1, D_MODEL)), _resident(wg.shape), _resident(wu.shape), _resident(wd.shape)],
        out_specs=row,
        out_shape=jax.ShapeDtypeStruct((s, D_MODEL), _F32),
        compiler_params=_params(1),
    )(x2d, gain, wg, wu, wd)


def _prep_call(x2d, gain, w_in, cos, sin, gqa, gka, gqb, gkb, seg):
    s = x2d.shape[0]
    nk = s // KV_CHUNK
    assert ROW_TILE == KV_CHUNK
    row = lambda width: pl.BlockSpec((ROW_TILE, width), lambda i: (i, 0))
    col = lambda height: pl.BlockSpec((height, ROW_TILE), lambda i: (0, i))
    vec = _resident((1, LANES))
    out_shape = (
        jax.ShapeDtypeStruct((A_Q_COLS, s), _BF16),
        jax.ShapeDtypeStruct((A_KV_HEADS, s, KA_LANES), _BF16),
        jax.ShapeDtypeStruct((A_KV_HEADS, nk, HEAD_DIM + BF16_SUBLANES, KV_CHUNK), _BF16),
        jax.ShapeDtypeStruct((B_QK_COLS, s), _BF16),
        jax.ShapeDtypeStruct((s, B_HEADS * KB_LANES), _BF16),
        jax.ShapeDtypeStruct((B_HEADS, nk, B_VDIM + BF16_SUBLANES, KV_CHUNK), _BF16),
        jax.ShapeDtypeStruct((STAT_ROWS, LANES), _F32),
    )
    out_specs = (
        col(A_Q_COLS),
        pl.BlockSpec((A_KV_HEADS, ROW_TILE, KA_LANES), lambda i: (0, i, 0)),
        pl.BlockSpec((A_KV_HEADS, 1, HEAD_DIM + BF16_SUBLANES, KV_CHUNK), lambda i: (0, i, 0, 0)),
        col(B_QK_COLS),
        row(B_HEADS * KB_LANES),
        pl.BlockSpec((B_HEADS, 1, B_VDIM + BF16_SUBLANES, KV_CHUNK), lambda i: (0, i, 0, 0)),
        pl.BlockSpec((STAT_ROWS, LANES), lambda i: (0, 0)),
    )
    return pl.pallas_call(
        _prep_body, name="in_proj_prep",
        grid=(s // ROW_TILE,),
        in_specs=[row(D_MODEL), _resident((1, D_MODEL)), _resident(w_in.shape), row(LANES), row(LANES),
                  vec, vec, vec, vec, _resident((LANES, LANES))],
        out_specs=out_specs,
        out_shape=out_shape,
        compiler_params=_params(1),
    )(x2d, gain, w_in, cos, sin, gqa, gka, gqb, gkb, seg)


def _attn_a_call(qta, ka, vta, stats, shift_path):
    s = qta.shape[1]
    nk = s // KV_CHUNK
    rows = A_GROUP * HEAD_DIM
    in_specs = [pl.BlockSpec((rows, A_Q_TOKENS), lambda g, i: (g, i)),
                pl.BlockSpec((None, s, KA_LANES), lambda g, i: (g, 0, 0)),
                pl.BlockSpec((None, nk, HEAD_DIM + BF16_SUBLANES, KV_CHUNK), lambda g, i: (g, 0, 0, 0))]
    common = dict(
        grid=(A_KV_HEADS, s // A_Q_TOKENS),
        out_specs=pl.BlockSpec((rows, A_Q_TOKENS), lambda g, i: (g, i)),
        out_shape=jax.ShapeDtypeStruct((A_Q_COLS, s), _BF16),
        compiler_params=_params(2))
    if shift_path:
        return pl.pallas_call(_attn_a_shift_body, name="attn_gqa_shift",
                              in_specs=in_specs + [pl.BlockSpec((STAT_ROWS, LANES), lambda g, i: (0, 0))],
                              scratch_shapes=[_score_scratch(A_GROUP * A_Q_TOKENS)], **common)(qta, ka, vta, stats)
    return pl.pallas_call(_attn_a_online_body, name="attn_gqa_online", in_specs=in_specs,
                          scratch_shapes=_online_scratch(A_GROUP * A_Q_TOKENS), **common)(qta, ka, vta)


def _attn_b_call(qtb, kb, vtb, stats, slopes, lq1, lk1, lq2, lk2, subln, lambda_init, shift_path):
    s = qtb.shape[1]
    nk = s // KV_CHUNK
    n = 2 * B_Q_TOKENS
    per_head = lambda width: pl.BlockSpec((None, 1, width), lambda h, i: (h, 0, 0))
    qkv_specs = [pl.BlockSpec((2 * HEAD_DIM, B_Q_TOKENS), lambda h, i: (h, i)),
                 pl.BlockSpec((s, KB_LANES), lambda h, i: (0, h)),
                 pl.BlockSpec((None, nk, B_VDIM + BF16_SUBLANES, KV_CHUNK), lambda h, i: (h, 0, 0, 0))]
    param_specs = [per_head(1), per_head(HEAD_DIM), per_head(HEAD_DIM), per_head(HEAD_DIM), per_head(HEAD_DIM),
                   pl.BlockSpec((B_VDIM, 1), lambda h, i: (0, 0))]
    common = dict(
        grid=(B_HEADS, s // B_Q_TOKENS),
        out_specs=pl.BlockSpec((B_VDIM, B_Q_TOKENS), lambda h, i: (h, i)),
        out_shape=jax.ShapeDtypeStruct((B_V_COLS, s), _BF16),
        compiler_params=_params(2))
    params = (slopes, lq1, lk1, lq2, lk2, subln)
    if shift_path:
        return pl.pallas_call(
            functools.partial(_attn_b_shift_body, lambda_init=lambda_init), name="attn_diff_shift",
            in_specs=qkv_specs + [pl.BlockSpec((STAT_ROWS, LANES), lambda h, i: (0, 0))] + param_specs,
            scratch_shapes=[pltpu.VMEM((3, KB_LANES, n), _BF16), _score_scratch(n)],
            **common)(qtb, kb, vtb, stats, *params)
    return pl.pallas_call(
        functools.partial(_attn_b_online_body, lambda_init=lambda_init), name="attn_diff_online",
        in_specs=qkv_specs + param_specs, scratch_shapes=_online_scratch(n), **common)(qtb, kb, vtb, *params)


def _out_call(x2d, at, bt, w_out, gain, wg, wu, wd):
    s = x2d.shape[0]
    row = pl.BlockSpec((ROW_TILE, D_MODEL), lambda i: (i, 0))
    col = pl.BlockSpec((at.shape[0], ROW_TILE), lambda i: (0, i))
    return pl.pallas_call(
        _out_body, name="out_proj_ffn",
        grid=(s // ROW_TILE,),
        in_specs=[row, col, col, _resident(w_out.shape), _resident((1, D_MODEL)),
                  _resident(wg.shape), _resident(wu.shape), _resident(wd.shape)],
        out_specs=row,
        out_shape=jax.ShapeDtypeStruct((s, D_MODEL), _F32),
        compiler_params=_params(1),
    )(x2d, at, bt, w_out, gain, wg, wu, wd)


def _rope_tables(s):
    rows = s // GRID_W
    row = jnp.broadcast_to(jnp.arange(rows, dtype=_F32)[:, None], (rows, GRID_W)).reshape(-1)
    colp = jnp.broadcast_to(jnp.arange(GRID_W, dtype=_F32)[None, :], (rows, GRID_W)).reshape(-1)
    inv_freq = ROPE_THETA ** (-jnp.arange(0, AX_DIM, 2, dtype=_F32) / AX_DIM)
    ang_r = row[:, None] * inv_freq[None, :]
    ang_c = colp[:, None] * inv_freq[None, :]
    cos = jnp.concatenate([jnp.cos(ang_r)] * 2 + [jnp.cos(ang_c)] * 2, axis=-1)
    sin = jnp.concatenate([-jnp.sin(ang_r), jnp.sin(ang_r), -jnp.sin(ang_c), jnp.sin(ang_c)], axis=-1)
    return jnp.tile(cos, (1, LANES // HEAD_DIM)), jnp.tile(sin, (1, LANES // HEAD_DIM))


def _max_score_bound(stats):
    chunks = stats[:, ::HEAD_DIM]
    a_q = chunks[STAT_AQ:STAT_AQ + A_HEADS // 2].reshape(A_KV_HEADS, A_GROUP)
    a = jnp.max(a_q * chunks[STAT_AK][:, None])
    b = jnp.max(chunks[STAT_BQ:STAT_BQ + B_HEADS] * chunks[STAT_BK:STAT_BK + B_HEADS])
    return jnp.sqrt(jnp.maximum(a, b)) * BOUND_MARGIN


def kernel(x, ffn1_norm, ffn1_w_gate, ffn1_w_up, ffn1_w_down, attn_norm, w_in, a_q_norm, a_k_norm, b_q_norm, b_k_norm, lambda_q1, lambda_k1, lambda_q2, lambda_k2, b_subln, w_out, ffn2_norm, ffn2_w_gate, ffn2_w_up, ffn2_w_down):
    bsz, s, _ = x.shape
    depth = w_in.shape[0]
    assert s % ROW_TILE == 0 and s % KV_CHUNK == 0 and s % A_Q_TOKENS == 0 and s % B_Q_TOKENS == 0
    assert KV_CHUNK % B_Q_TOKENS == 0 and s < POS_SPLIT * 256
    cos, sin = _rope_tables(s)
    lane_chunk = jnp.arange(LANES) // HEAD_DIM
    seg = ((lane_chunk[:, None] == lane_chunk[None, :]).astype(_F32) / HEAD_DIM).astype(_BF16)
    slopes = (2.0 ** (-8.0 * (jnp.arange(B_HEADS, dtype=_F32) + 1.0) / B_HEADS)).reshape(B_HEADS, 1, 1)
    lane_gain = lambda g: jnp.tile(g, LANES // HEAD_DIM).reshape(1, LANES)
    bf = lambda w: w.astype(_BF16)

    outs = []
    for b in range(bsz):
        xb = x[b]
        for l in range(depth):
            lambda_init = 0.8 - 0.6 * math.exp(-0.3 * l)
            xb = _ffn_call(xb, ffn1_norm[l].reshape(1, D_MODEL), bf(ffn1_w_gate[l]), bf(ffn1_w_up[l]), bf(ffn1_w_down[l]))
            qta, ka, vta, qtb, kb, vtb, stats = _prep_call(
                xb, attn_norm[l].reshape(1, D_MODEL), bf(w_in[l]), cos, sin,
                lane_gain(a_q_norm[l]), lane_gain(a_k_norm[l]), lane_gain(b_q_norm[l]), lane_gain(b_k_norm[l]), seg)
            per_head = lambda p: p[l].reshape(B_HEADS, 1, HEAD_DIM)
            b_params = (slopes, per_head(lambda_q1), per_head(lambda_k1), per_head(lambda_q2), per_head(lambda_k2),
                        b_subln[l].reshape(B_VDIM, 1), lambda_init)

            def mixers(shift_path, qta=qta, ka=ka, vta=vta, qtb=qtb, kb=kb, vtb=vtb, stats=stats, b_params=b_params):
                return (_attn_a_call(qta, ka, vta, stats, shift_path),
                        _attn_b_call(qtb, kb, vtb, stats, *b_params, shift_path))

            at, bt = lax.cond(_max_score_bound(stats) <= SHIFT_PATH_MAX_BOUND,
                              lambda: mixers(True), lambda: mixers(False))
            xb = _out_call(xb, at, bt, bf(w_out[l]), ffn2_norm[l].reshape(1, D_MODEL),
                           bf(ffn2_w_gate[l]), bf(ffn2_w_up[l]), bf(ffn2_w_down[l]))
        outs.append(xb)
    return jnp.stack(outs, axis=0)
```

```python
import functools
import math

import jax
import jax.numpy as jnp
from jax import lax
from jax.experimental import pallas as pl
from jax.experimental.pallas import tpu as pltpu

D_MODEL = 1024
HEAD_DIM = 64
GRID_W = 64
EPS = 1e-6
ROPE_THETA = 10000.0
AX_DIM = HEAD_DIM // 2
A_HEADS = 8
A_KV_HEADS = 2
A_GROUP = A_HEADS // A_KV_HEADS
B_HEADS = 4
B_VDIM = 2 * HEAD_DIM
A_Q_COLS = A_HEADS * HEAD_DIM
A_KV_COLS = A_KV_HEADS * HEAD_DIM
B_QK_COLS = B_HEADS * 2 * HEAD_DIM
B_V_COLS = B_HEADS * B_VDIM
IN_COLS = A_Q_COLS + 2 * A_KV_COLS + 2 * B_QK_COLS + B_V_COLS
D_FF = 2816

LANES = 128
BF16_SUBLANES = 16
ROW_TILE = 512
KV_CHUNK = 512
A_Q_TOKENS = 128
B_Q_TOKENS = 256
FF_SPLITS = (0, 1536, D_FF)
VMEM_LIMIT = 56 * 1024 * 1024
NEG_BIG = -1e30

KA_LANES = 2 * HEAD_DIM
KB_LANES = 4 * HEAD_DIM
POS_SPLIT = 128
BOUND_MARGIN = 1.0 + 2.0 ** -6
SHIFT_PATH_MAX_BOUND = 40.0
DEAD_SHIFT = 256.0

_F32 = jnp.float32
_BF16 = jnp.bfloat16


def _dot(a, b):
    return jnp.dot(a, b, preferred_element_type=_F32)


def _rms_rows(x, gain):
    return x * lax.rsqrt(jnp.mean(x * x, axis=-1, keepdims=True) + EPS) * gain


def _swiglu_half_step(x, gain, wg_ref, wu_ref, wd_ref):
    h = _rms_rows(x, gain).astype(_BF16)
    acc = jnp.zeros(x.shape, _F32)
    for lo, hi in zip(FF_SPLITS[:-1], FF_SPLITS[1:]):
        cols = slice(lo, hi)
        g = _dot(h, wg_ref[:, cols])
        u = _dot(h, wu_ref[:, cols])
        a = (g / (1.0 + jnp.exp(-g)) * u).astype(_BF16)
        acc = acc + _dot(a, wd_ref[cols, :])
    return x + 0.5 * acc


def _ffn_body(x_ref, gain_ref, wg_ref, wu_ref, wd_ref, o_ref):
    o_ref[...] = _swiglu_half_step(x_ref[...], gain_ref[...], wg_ref, wu_ref, wd_ref)


def _chunk_sums(x, seg):
    hi = x.astype(_BF16)
    lo = (x - hi.astype(_F32)).astype(_BF16)
    return _dot(hi, seg) + _dot(lo, seg)


def _prep_body(x_ref, gain_ref, w_in_ref, cos_ref, sin_ref, gqa_ref, gka_ref, gqb_ref, gkb_ref, seg_ref,
               qta_ref, ka_ref, vta_ref, qtb_ref, kb_ref, vtb_ref):
    rows = x_ref.shape[0]
    h = _rms_rows(x_ref[...], gain_ref[...]).astype(_BF16)
    proj = _dot(h, w_in_ref[...])
    seg = seg_ref[...]
    scale = 1.0 / math.sqrt(HEAD_DIM)

    def block(col):
        return proj[:, col:col + LANES]

    def qk_norm(blk, gain):
        return blk * lax.rsqrt(_chunk_sums(blk * blk, seg) + EPS) * gain

    lane = lax.broadcasted_iota(jnp.int32, (rows, LANES), 1)
    first_half = (lane % AX_DIM) < (AX_DIM // 2)
    cos = cos_ref[...]
    sin = sin_ref[...]

    def rope(blk):
        partner = jnp.where(first_half, pltpu.roll(blk, LANES - AX_DIM // 2, 1), pltpu.roll(blk, AX_DIM // 2, 1))
        return blk * cos + partner * sin

    for b in range(A_Q_COLS // LANES):
        q = rope(qk_norm(block(b * LANES), gqa_ref[...])) * scale
        qta_ref[b * LANES:(b + 1) * LANES, :] = q.T.astype(_BF16)
    k = rope(qk_norm(block(A_Q_COLS), gka_ref[...])).astype(_BF16)
    vt = block(A_Q_COLS + A_KV_COLS).T.astype(_BF16)
    ones_rows = jnp.ones((BF16_SUBLANES, rows), _BF16)
    one_lane = (lax.broadcasted_iota(jnp.int32, (rows, HEAD_DIM), 1) == 0).astype(_BF16)
    for g in range(A_KV_HEADS):
        ka_ref[g] = jnp.concatenate([k[:, g * HEAD_DIM:(g + 1) * HEAD_DIM], one_lane], axis=1)
        vta_ref[g, 0, 0:HEAD_DIM, :] = vt[g * HEAD_DIM:(g + 1) * HEAD_DIM, :]
        vta_ref[g, 0, HEAD_DIM:HEAD_DIM + BF16_SUBLANES, :] = ones_rows
    pos = pl.program_id(0) * rows + lax.broadcasted_iota(jnp.int32, (rows, LANES), 0)
    pos_feat = jnp.where(lane == 0, (pos // POS_SPLIT * POS_SPLIT).astype(_F32),
                         jnp.where(lane == 1, (pos % POS_SPLIT).astype(_F32),
                                   jnp.where(lane < 5, 1.0, 0.0))).astype(_BF16)
    b_q0 = A_Q_COLS + 2 * A_KV_COLS
    b_k0 = b_q0 + B_QK_COLS
    b_v0 = b_k0 + B_QK_COLS
    for hd in range(B_HEADS):
        q = qk_norm(block(b_q0 + hd * LANES), gqb_ref[...]) * scale
        qtb_ref[hd * LANES:(hd + 1) * LANES, :] = q.T.astype(_BF16)
        kb_ref[:, hd * KB_LANES:hd * KB_LANES + LANES] = qk_norm(block(b_k0 + hd * LANES), gkb_ref[...]).astype(_BF16)
        kb_ref[:, hd * KB_LANES + LANES:(hd + 1) * KB_LANES] = pos_feat
        vtb_ref[hd, 0, 0:B_VDIM, :] = block(b_v0 + hd * LANES).T.astype(_BF16)
        vtb_ref[hd, 0, B_VDIM:B_VDIM + BF16_SUBLANES, :] = ones_rows


def _online_columns(w, k_ref, vt_ref, s_scr, p_scr, bias_fn):
    n = w.shape[1]
    rows = vt_ref.shape[1]
    nk = vt_ref.shape[0]
    assert nk % 2 == 0

    def scores(j, slot):
        k = k_ref[pl.ds(pl.multiple_of(j * KV_CHUNK, KV_CHUNK), KV_CHUNK), :]
        s = _dot(k, w)
        if bias_fn is not None:
            s = s - bias_fn(j)
        s_scr[slot] = s
        return jnp.max(s, axis=0, keepdims=True)

    def weights(slot, m, cmax):
        m_new = jnp.maximum(m, cmax)
        p_scr[slot] = jnp.exp(s_scr[slot] - m_new).astype(_BF16)
        return m_new, jnp.exp(m - m_new)

    def values(j, slot, acc, alpha):
        return alpha * acc + _dot(vt_ref[j], p_scr[slot])

    m = jnp.full((1, n), NEG_BIG, _F32)
    acc = jnp.zeros((rows, n), _F32)
    cmax0 = scores(0, 0)
    cmax1 = scores(1, 1)
    m, alpha = weights(0, m, cmax0)

    def body(i, carry):
        m, alpha, cmax1, acc = carry
        j = 2 * i + 1
        cmax0 = scores(j + 1, 0)
        acc = values(j - 1, 0, acc, alpha)
        m, alpha = weights(1, m, cmax1)
        cmax1 = scores(j + 2, 1)
        acc = values(j, 1, acc, alpha)
        m, alpha = weights(0, m, cmax0)
        return m, alpha, cmax1, acc

    m, alpha, cmax1, acc = lax.fori_loop(0, nk // 2 - 1, body, (m, alpha, cmax1, acc))
    acc = values(nk - 2, 0, acc, alpha)
    m, alpha = weights(1, m, cmax1)
    return values(nk - 1, 1, acc, alpha)


def _sublane_partial_sums(p):
    return jnp.sum(p.reshape(p.shape[0] // 8, 8, p.shape[1]), axis=0)


def _shifted_columns(w_fn, k_ref, vt_ref, s_scr, acc, denom):
    nk = vt_ref.shape[0]
    v_rows = acc.shape[0]

    def scores(j, slot):
        k = k_ref[pl.ds(j * KV_CHUNK, KV_CHUNK), :]
        s_scr[slot] = _dot(k, w_fn(j))

    scores(0, 0)
    for c in range(nk):
        if c + 1 < nk:
            scores(c + 1, (c + 1) % 2)
        p = jnp.exp(s_scr[c % 2])
        if denom is not None:
            denom = denom + _sublane_partial_sums(p)
        acc = acc + _dot(vt_ref[c, 0:v_rows, :], p.astype(_BF16))
    return acc, denom


def _feature_rows(rows_list, n):
    row = lax.broadcasted_iota(jnp.int32, (BF16_SUBLANES, n), 0)
    tile = jnp.zeros((BF16_SUBLANES, n), _F32)
    for r, v in enumerate(rows_list):
        tile = jnp.where(row == r, v, tile)
    return tile.astype(_BF16)


def _a_queries(qt_ref):
    return jnp.concatenate([qt_ref[r * HEAD_DIM:(r + 1) * HEAD_DIM, :] for r in range(A_GROUP)], axis=1)


def _a_finish(acc, o_ref):
    tq = o_ref.shape[1]
    out = acc[0:HEAD_DIM, :] / acc[HEAD_DIM:HEAD_DIM + 1, :]
    for r in range(A_GROUP):
        o_ref[r * HEAD_DIM:(r + 1) * HEAD_DIM, :] = out[:, r * tq:(r + 1) * tq].astype(o_ref.dtype)


def _attn_a_shift_body(qt_ref, k_ref, vt_ref, bound_ref, o_ref, s_scr):
    wq = _a_queries(qt_ref)
    n = wq.shape[1]
    feat = _feature_rows([-jnp.broadcast_to(bound_ref[...], (1, n))], n)
    w = jnp.concatenate([wq, feat, jnp.zeros((KA_LANES - HEAD_DIM - BF16_SUBLANES, n), _BF16)], axis=0)
    acc, _ = _shifted_columns(lambda j: w, k_ref, vt_ref, s_scr, jnp.zeros((vt_ref.shape[1], n), _F32), None)
    _a_finish(acc, o_ref)


def _attn_a_online_body(qt_ref, k_ref, vt_ref, o_ref, s_scr, p_scr):
    wq = _a_queries(qt_ref)
    w = jnp.concatenate([wq, jnp.zeros((KA_LANES - HEAD_DIM, wq.shape[1]), _BF16)], axis=0)
    _a_finish(_online_columns(w, k_ref, vt_ref, s_scr, p_scr, None), o_ref)


def _b_queries(qt_ref):
    tq = qt_ref.shape[1]
    zeros = jnp.zeros((HEAD_DIM, tq), _BF16)
    return jnp.concatenate([jnp.concatenate([qt_ref[0:HEAD_DIM, :], zeros], axis=1),
                            jnp.concatenate([zeros, qt_ref[HEAD_DIM:2 * HEAD_DIM, :]], axis=1)], axis=0)


def _b_finish(acc, denom, lq1_ref, lk1_ref, lq2_ref, lk2_ref, subln_ref, o_ref, lambda_init):
    tq = o_ref.shape[1]
    lam = (jnp.exp(jnp.sum(lq1_ref[...] * lk1_ref[...], axis=-1, keepdims=True))
           - jnp.exp(jnp.sum(lq2_ref[...] * lk2_ref[...], axis=-1, keepdims=True)) + lambda_init)
    o0 = acc[0:B_VDIM, 0:tq] / denom[:, 0:tq]
    o1 = acc[0:B_VDIM, tq:2 * tq] / denom[:, tq:2 * tq]
    o = o0 - lam * o1
    o = o * lax.rsqrt(jnp.mean(o * o, axis=0, keepdims=True) + EPS) * subln_ref[...] * (1.0 - lambda_init)
    o_ref[...] = o.astype(o_ref.dtype)


def _attn_b_shift_body(qt_ref, k_ref, vt_ref, bound_ref, slope_ref, lq1_ref, lk1_ref, lq2_ref, lk2_ref,
                       subln_ref, o_ref, w_scr, s_scr, *, lambda_init):
    tq = qt_ref.shape[1]
    n = 2 * tq
    wq = _b_queries(qt_ref)
    bound = jnp.broadcast_to(bound_ref[...], (1, n))
    slope = jnp.broadcast_to(slope_ref[...], (1, n))
    q_pos = pl.program_id(1) * tq + lax.broadcasted_iota(jnp.int32, (1, tq), 1)
    q_hi = (q_pos // POS_SPLIT * POS_SPLIT).astype(_F32)
    q_lo = (q_pos % POS_SPLIT).astype(_F32)
    sq_hi = slope * jnp.concatenate([q_hi, q_hi], axis=1)
    sq_lo = slope * jnp.concatenate([q_lo, q_lo], axis=1)
    pad = jnp.zeros((KB_LANES - 2 * HEAD_DIM - BF16_SUBLANES, n), _BF16)
    zero = jnp.zeros((1, n), _F32)
    w_scr[0] = jnp.concatenate([wq, _feature_rows([slope, slope, -sq_hi, -sq_lo, -bound], n), pad], axis=0)
    w_scr[1] = jnp.concatenate([wq, _feature_rows([-slope, -slope, sq_hi, sq_lo, -bound], n), pad], axis=0)
    w_scr[2] = jnp.concatenate([wq, _feature_rows([zero, zero, zero, zero, -bound - DEAD_SHIFT], n), pad], axis=0)

    diag = (pl.program_id(1) * tq) // KV_CHUNK
    k_diag = k_ref[pl.ds(pl.multiple_of(diag * KV_CHUNK, KV_CHUNK), KV_CHUNK), :]
    p = jnp.exp(jnp.minimum(_dot(k_diag, w_scr[0]), _dot(k_diag, w_scr[1])))
    acc = _dot(vt_ref[diag, 0:B_VDIM, :], p.astype(_BF16))
    denom = _sublane_partial_sums(p)

    def w_fn(j):
        return w_scr[jnp.where(j < diag, 0, jnp.where(j > diag, 1, 2))]

    acc, denom = _shifted_columns(w_fn, k_ref, vt_ref, s_scr, acc, denom)
    _b_finish(acc, jnp.sum(denom, axis=0, keepdims=True), lq1_ref, lk1_ref, lq2_ref, lk2_ref, subln_ref, o_ref,
              lambda_init)


def _attn_b_online_body(qt_ref, k_ref, vt_ref, slope_ref, lq1_ref, lk1_ref, lq2_ref, lk2_ref, subln_ref, o_ref,
                        s_scr, p_scr, *, lambda_init):
    tq = qt_ref.shape[1]
    wq = _b_queries(qt_ref)
    w = jnp.concatenate([wq, jnp.zeros((KB_LANES - 2 * HEAD_DIM, 2 * tq), _BF16)], axis=0)
    slope = slope_ref[...]
    q_pos = pl.program_id(1) * tq + lax.broadcasted_iota(jnp.int32, (KV_CHUNK, tq), 1)
    k_off = lax.broadcasted_iota(jnp.int32, (KV_CHUNK, tq), 0)

    def bias_fn(j):
        dist = jnp.abs(q_pos - (k_off + j * KV_CHUNK)).astype(_F32)
        bias = slope * dist
        return jnp.concatenate([bias, bias], axis=1)

    acc = _online_columns(w, k_ref, vt_ref, s_scr, p_scr, bias_fn)
    _b_finish(acc, acc[B_VDIM:B_VDIM + 1, :], lq1_ref, lk1_ref, lq2_ref, lk2_ref, subln_ref, o_ref, lambda_init)


def _out_body(x_ref, at_ref, bt_ref, w_out_ref, gain_ref, wg_ref, wu_ref, wd_ref, o_ref):
    tn = (((0,), (0,)), ((), ()))
    half = at_ref.shape[0]
    y = (lax.dot_general(at_ref[...], w_out_ref[0:half, :], tn, preferred_element_type=_F32)
         + lax.dot_general(bt_ref[...], w_out_ref[half:2 * half, :], tn, preferred_element_type=_F32))
    x = x_ref[...] + y
    o_ref[...] = _swiglu_half_step(x, gain_ref[...], wg_ref, wu_ref, wd_ref)


def _resident(shape):
    return pl.BlockSpec(shape, lambda *_: (0,) * len(shape), pipeline_mode=pl.Buffered(1))


def _score_scratch(n):
    return pltpu.VMEM((2, KV_CHUNK, n), _F32)


def _online_scratch(n):
    return [_score_scratch(n), pltpu.VMEM((2, KV_CHUNK, n), _BF16)]


def _params(n_axes):
    return pltpu.CompilerParams(dimension_semantics=("arbitrary",) * n_axes, vmem_limit_bytes=VMEM_LIMIT)


def _ffn_call(x2d, gain, wg, wu, wd):
    s = x2d.shape[0]
    row = pl.BlockSpec((ROW_TILE, D_MODEL), lambda i: (i, 0))
    return pl.pallas_call(
        _ffn_body, name="ffn_half_step",
        grid=(s // ROW_TILE,),
        in_specs=[row, _resident((1, D_MODEL)), _resident(wg.shape), _resident(wu.shape), _resident(wd.shape)],
        out_specs=row,
        out_shape=jax.ShapeDtypeStruct((s, D_MODEL), _F32),
        compiler_params=_params(1),
    )(x2d, gain, wg, wu, wd)


def _prep_call(x2d, gain, w_in, cos, sin, gqa, gka, gqb, gkb, seg):
    s = x2d.shape[0]
    nk = s // KV_CHUNK
    assert ROW_TILE == KV_CHUNK
    row = lambda width: pl.BlockSpec((ROW_TILE, width), lambda i: (i, 0))
    col = lambda height: pl.BlockSpec((height, ROW_TILE), lambda i: (0, i))
    vec = _resident((1, LANES))
    out_shape = (
        jax.ShapeDtypeStruct((A_Q_COLS, s), _BF16),
        jax.ShapeDtypeStruct((A_KV_HEADS, s, KA_LANES), _BF16),
        jax.ShapeDtypeStruct((A_KV_HEADS, nk, HEAD_DIM + BF16_SUBLANES, KV_CHUNK), _BF16),
        jax.ShapeDtypeStruct((B_QK_COLS, s), _BF16),
        jax.ShapeDtypeStruct((s, B_HEADS * KB_LANES), _BF16),
        jax.ShapeDtypeStruct((B_HEADS, nk, B_VDIM + BF16_SUBLANES, KV_CHUNK), _BF16),
    )
    out_specs = (
        col(A_Q_COLS),
        pl.BlockSpec((A_KV_HEADS, ROW_TILE, KA_LANES), lambda i: (0, i, 0)),
        pl.BlockSpec((A_KV_HEADS, 1, HEAD_DIM + BF16_SUBLANES, KV_CHUNK), lambda i: (0, i, 0, 0)),
        col(B_QK_COLS),
        row(B_HEADS * KB_LANES),
        pl.BlockSpec((B_HEADS, 1, B_VDIM + BF16_SUBLANES, KV_CHUNK), lambda i: (0, i, 0, 0)),
    )
    return pl.pallas_call(
        _prep_body, name="in_proj_prep",
        grid=(s // ROW_TILE,),
        in_specs=[row(D_MODEL), _resident((1, D_MODEL)), _resident(w_in.shape), row(LANES), row(LANES),
                  vec, vec, vec, vec, _resident((LANES, LANES))],
        out_specs=out_specs,
        out_shape=out_shape,
        compiler_params=_params(1),
    )(x2d, gain, w_in, cos, sin, gqa, gka, gqb, gkb, seg)


def _attn_a_call(qta, ka, vta, bound, shift_path):
    s = qta.shape[1]
    nk = s // KV_CHUNK
    rows = A_GROUP * HEAD_DIM
    in_specs = [pl.BlockSpec((rows, A_Q_TOKENS), lambda g, i: (g, i)),
                pl.BlockSpec((None, s, KA_LANES), lambda g, i: (g, 0, 0)),
                pl.BlockSpec((None, nk, HEAD_DIM + BF16_SUBLANES, KV_CHUNK), lambda g, i: (g, 0, 0, 0))]
    common = dict(
        grid=(A_KV_HEADS, s // A_Q_TOKENS),
        out_specs=pl.BlockSpec((rows, A_Q_TOKENS), lambda g, i: (g, i)),
        out_shape=jax.ShapeDtypeStruct((A_Q_COLS, s), _BF16),
        compiler_params=_params(2))
    if shift_path:
        return pl.pallas_call(_attn_a_shift_body, name="attn_gqa_shift",
                              in_specs=in_specs + [pl.BlockSpec((1, 1), lambda g, i: (0, 0))],
                              scratch_shapes=[_score_scratch(A_GROUP * A_Q_TOKENS)], **common)(qta, ka, vta, bound)
    return pl.pallas_call(_attn_a_online_body, name="attn_gqa_online", in_specs=in_specs,
                          scratch_shapes=_online_scratch(A_GROUP * A_Q_TOKENS), **common)(qta, ka, vta)


def _attn_b_call(qtb, kb, vtb, bound, slopes, lq1, lk1, lq2, lk2, subln, lambda_init, shift_path):
    s = qtb.shape[1]
    nk = s // KV_CHUNK
    n = 2 * B_Q_TOKENS
    per_head = lambda width: pl.BlockSpec((None, 1, width), lambda h, i: (h, 0, 0))
    qkv_specs = [pl.BlockSpec((2 * HEAD_DIM, B_Q_TOKENS), lambda h, i: (h, i)),
                 pl.BlockSpec((s, KB_LANES), lambda h, i: (0, h)),
                 pl.BlockSpec((None, nk, B_VDIM + BF16_SUBLANES, KV_CHUNK), lambda h, i: (h, 0, 0, 0))]
    param_specs = [per_head(1), per_head(HEAD_DIM), per_head(HEAD_DIM), per_head(HEAD_DIM), per_head(HEAD_DIM),
                   pl.BlockSpec((B_VDIM, 1), lambda h, i: (0, 0))]
    common = dict(
        grid=(B_HEADS, s // B_Q_TOKENS),
        out_specs=pl.BlockSpec((B_VDIM, B_Q_TOKENS), lambda h, i: (h, i)),
        out_shape=jax.ShapeDtypeStruct((B_V_COLS, s), _BF16),
        compiler_params=_params(2))
    params = (slopes, lq1, lk1, lq2, lk2, subln)
    if shift_path:
        return pl.pallas_call(
            functools.partial(_attn_b_shift_body, lambda_init=lambda_init), name="attn_diff_shift",
            in_specs=qkv_specs + [pl.BlockSpec((1, 1), lambda h, i: (0, 0))] + param_specs,
            scratch_shapes=[pltpu.VMEM((3, KB_LANES, n), _BF16), _score_scratch(n)],
            **common)(qtb, kb, vtb, bound, *params)
    return pl.pallas_call(
        functools.partial(_attn_b_online_body, lambda_init=lambda_init), name="attn_diff_online",
        in_specs=qkv_specs + param_specs, scratch_shapes=_online_scratch(n), **common)(qtb, kb, vtb, *params)


def _out_call(x2d, at, bt, w_out, gain, wg, wu, wd):
    s = x2d.shape[0]
    row = pl.BlockSpec((ROW_TILE, D_MODEL), lambda i: (i, 0))
    col = pl.BlockSpec((at.shape[0], ROW_TILE), lambda i: (0, i))
    return pl.pallas_call(
        _out_body, name="out_proj_ffn",
        grid=(s // ROW_TILE,),
        in_specs=[row, col, col, _resident(w_out.shape), _resident((1, D_MODEL)),
                  _resident(wg.shape), _resident(wu.shape), _resident(wd.shape)],
        out_specs=row,
        out_shape=jax.ShapeDtypeStruct((s, D_MODEL), _F32),
        compiler_params=_params(1),
    )(x2d, at, bt, w_out, gain, wg, wu, wd)


def _rope_tables(s):
    rows = s // GRID_W
    row = jnp.broadcast_to(jnp.arange(rows, dtype=_F32)[:, None], (rows, GRID_W)).reshape(-1)
    colp = jnp.broadcast_to(jnp.arange(GRID_W, dtype=_F32)[None, :], (rows, GRID_W)).reshape(-1)
    inv_freq = ROPE_THETA ** (-jnp.arange(0, AX_DIM, 2, dtype=_F32) / AX_DIM)
    ang_r = row[:, None] * inv_freq[None, :]
    ang_c = colp[:, None] * inv_freq[None, :]
    cos = jnp.concatenate([jnp.cos(ang_r)] * 2 + [jnp.cos(ang_c)] * 2, axis=-1)
    sin = jnp.concatenate([-jnp.sin(ang_r), jnp.sin(ang_r), -jnp.sin(ang_c), jnp.sin(ang_c)], axis=-1)
    return jnp.tile(cos, (1, LANES // HEAD_DIM)), jnp.tile(sin, (1, LANES // HEAD_DIM))


def _score_bound(q_gain, k_gain):
    bound = math.sqrt(HEAD_DIM) * jnp.max(jnp.abs(q_gain)) * jnp.max(jnp.abs(k_gain)) * BOUND_MARGIN
    return bound.astype(_F32).reshape(1, 1)


def kernel(x, ffn1_norm, ffn1_w_gate, ffn1_w_up, ffn1_w_down, attn_norm, w_in, a_q_norm, a_k_norm, b_q_norm, b_k_norm, lambda_q1, lambda_k1, lambda_q2, lambda_k2, b_subln, w_out, ffn2_norm, ffn2_w_gate, ffn2_w_up, ffn2_w_down):
    bsz, s, _ = x.shape
    depth = w_in.shape[0]
    assert s % ROW_TILE == 0 and s % KV_CHUNK == 0 and s % A_Q_TOKENS == 0 and s % B_Q_TOKENS == 0
    assert KV_CHUNK % B_Q_TOKENS == 0 and s < POS_SPLIT * 256
    cos, sin = _rope_tables(s)
    lane_chunk = jnp.arange(LANES) // HEAD_DIM
    seg = ((lane_chunk[:, None] == lane_chunk[None, :]).astype(_F32) / HEAD_DIM).astype(_BF16)
    slopes = (2.0 ** (-8.0 * (jnp.arange(B_HEADS, dtype=_F32) + 1.0) / B_HEADS)).reshape(B_HEADS, 1, 1)
    lane_gain = lambda g: jnp.tile(g, LANES // HEAD_DIM).reshape(1, LANES)
    bf = lambda w: w.astype(_BF16)

    outs = []
    for b in range(bsz):
        xb = x[b]
        for l in range(depth):
            lambda_init = 0.8 - 0.6 * math.exp(-0.3 * l)
            xb = _ffn_call(xb, ffn1_norm[l].reshape(1, D_MODEL), bf(ffn1_w_gate[l]), bf(ffn1_w_up[l]), bf(ffn1_w_down[l]))
            qta, ka, vta, qtb, kb, vtb = _prep_call(
                xb, attn_norm[l].reshape(1, D_MODEL), bf(w_in[l]), cos, sin,
                lane_gain(a_q_norm[l]), lane_gain(a_k_norm[l]), lane_gain(b_q_norm[l]), lane_gain(b_k_norm[l]), seg)
            per_head = lambda p: p[l].reshape(B_HEADS, 1, HEAD_DIM)
            b_params = (slopes, per_head(lambda_q1), per_head(lambda_k1), per_head(lambda_q2), per_head(lambda_k2),
                        b_subln[l].reshape(B_VDIM, 1), lambda_init)

            bound_a = _score_bound(a_q_norm[l], a_k_norm[l])
            bound_b = _score_bound(b_q_norm[l], b_k_norm[l])

            def mixers(shift_path, qta=qta, ka=ka, vta=vta, qtb=qtb, kb=kb, vtb=vtb, b_params=b_params,
                       bound_a=bound_a, bound_b=bound_b):
                return (_attn_a_call(qta, ka, vta, bound_a, shift_path),
                        _attn_b_call(qtb, kb, vtb, bound_b, *b_params, shift_path))

            at, bt = lax.cond(jnp.maximum(bound_a, bound_b)[0, 0] <= SHIFT_PATH_MAX_BOUND,
                              lambda: mixers(True), lambda: mixers(False))
            xb = _out_call(xb, at, bt, bf(w_out[l]), ffn2_norm[l].reshape(1, D_MODEL),
                           bf(ffn2_w_gate[l]), bf(ffn2_w_up[l]), bf(ffn2_w_down[l]))
        outs.append(xb)
    return jnp.stack(outs, axis=0)
```

```python
import functools
import math

import jax
import jax.numpy as jnp
from jax import lax
from jax.experimental import pallas as pl
from jax.experimental.pallas import tpu as pltpu

D_MODEL = 1024
HEAD_DIM = 64
GRID_W = 64
EPS = 1e-6
ROPE_THETA = 10000.0
AX_DIM = HEAD_DIM // 2
A_HEADS = 8
A_KV_HEADS = 2
A_GROUP = A_HEADS // A_KV_HEADS
B_HEADS = 4
B_VDIM = 2 * HEAD_DIM
A_Q_COLS = A_HEADS * HEAD_DIM
A_KV_COLS = A_KV_HEADS * HEAD_DIM
B_QK_COLS = B_HEADS * 2 * HEAD_DIM
B_V_COLS = B_HEADS * B_VDIM
IN_COLS = A_Q_COLS + 2 * A_KV_COLS + 2 * B_QK_COLS + B_V_COLS
D_FF = 2816

LANES = 128
BF16_SUBLANES = 16
ROW_TILE = 512
KV_CHUNK = 512
A_Q_TOKENS = 128
B_Q_TOKENS = 256
TILES_PER_STEP = 2
FF_SPLITS = (0, 1536, D_FF)
VMEM_LIMIT = 56 * 1024 * 1024
NEG_BIG = -1e30

KA_LANES = 2 * HEAD_DIM
KB_LANES = 4 * HEAD_DIM
POS_SPLIT = 128
BOUND_MARGIN = 1.0 + 2.0 ** -6
SHIFT_PATH_MAX_BOUND = 40.0
DEAD_SHIFT = 256.0

_F32 = jnp.float32
_BF16 = jnp.bfloat16


def _dot(a, b):
    return jnp.dot(a, b, preferred_element_type=_F32)


def _rms_rows(x, gain):
    return x * lax.rsqrt(jnp.mean(x * x, axis=-1, keepdims=True) + EPS) * gain


def _swiglu_half_step(x, gain, wg_ref, wu_ref, wd_ref):
    h = _rms_rows(x, gain).astype(_BF16)
    acc = jnp.zeros(x.shape, _F32)
    for lo, hi in zip(FF_SPLITS[:-1], FF_SPLITS[1:]):
        cols = slice(lo, hi)
        g = _dot(h, wg_ref[:, cols])
        u = _dot(h, wu_ref[:, cols])
        a = (g / (1.0 + jnp.exp(-g)) * u).astype(_BF16)
        acc = acc + _dot(a, wd_ref[cols, :])
    return x + 0.5 * acc


def _ffn_body(x_ref, gain_ref, wg_ref, wu_ref, wd_ref, o_ref):
    o_ref[...] = _swiglu_half_step(x_ref[...], gain_ref[...], wg_ref, wu_ref, wd_ref)


def _chunk_sums(x, seg):
    hi = x.astype(_BF16)
    lo = (x - hi.astype(_F32)).astype(_BF16)
    return _dot(hi, seg) + _dot(lo, seg)


def _prep_body(x_ref, gain_ref, w_in_ref, cos_ref, sin_ref, gqa_ref, gka_ref, gqb_ref, gkb_ref, seg_ref,
               qta_ref, ka_ref, vta_ref, qtb_ref, kb_ref, vtb_ref):
    rows = x_ref.shape[0]
    h = _rms_rows(x_ref[...], gain_ref[...]).astype(_BF16)
    proj = _dot(h, w_in_ref[...])
    seg = seg_ref[...]
    scale = 1.0 / math.sqrt(HEAD_DIM)

    def block(col):
        return proj[:, col:col + LANES]

    def qk_norm(blk, gain):
        return blk * lax.rsqrt(_chunk_sums(blk * blk, seg) + EPS) * gain

    lane = lax.broadcasted_iota(jnp.int32, (rows, LANES), 1)
    first_half = (lane % AX_DIM) < (AX_DIM // 2)
    cos = cos_ref[...]
    sin = sin_ref[...]

    def rope(blk):
        partner = jnp.where(first_half, pltpu.roll(blk, LANES - AX_DIM // 2, 1), pltpu.roll(blk, AX_DIM // 2, 1))
        return blk * cos + partner * sin

    for b in range(A_Q_COLS // LANES):
        q = rope(qk_norm(block(b * LANES), gqa_ref[...])) * scale
        qta_ref[b * LANES:(b + 1) * LANES, :] = q.T.astype(_BF16)
    k = rope(qk_norm(block(A_Q_COLS), gka_ref[...])).astype(_BF16)
    vt = block(A_Q_COLS + A_KV_COLS).T.astype(_BF16)
    ones_rows = jnp.ones((BF16_SUBLANES, rows), _BF16)
    one_lane = (lax.broadcasted_iota(jnp.int32, (rows, HEAD_DIM), 1) == 0).astype(_BF16)
    for g in range(A_KV_HEADS):
        ka_ref[g] = jnp.concatenate([k[:, g * HEAD_DIM:(g + 1) * HEAD_DIM], one_lane], axis=1)
        vta_ref[g, 0, 0:HEAD_DIM, :] = vt[g * HEAD_DIM:(g + 1) * HEAD_DIM, :]
        vta_ref[g, 0, HEAD_DIM:HEAD_DIM + BF16_SUBLANES, :] = ones_rows
    pos = pl.program_id(0) * rows + lax.broadcasted_iota(jnp.int32, (rows, LANES), 0)
    pos_feat = jnp.where(lane == 0, (pos // POS_SPLIT * POS_SPLIT).astype(_F32),
                         jnp.where(lane == 1, (pos % POS_SPLIT).astype(_F32),
                                   jnp.where(lane < 5, 1.0, 0.0))).astype(_BF16)
    b_q0 = A_Q_COLS + 2 * A_KV_COLS
    b_k0 = b_q0 + B_QK_COLS
    b_v0 = b_k0 + B_QK_COLS
    for hd in range(B_HEADS):
        q = qk_norm(block(b_q0 + hd * LANES), gqb_ref[...]) * scale
        qtb_ref[hd * LANES:(hd + 1) * LANES, :] = q.T.astype(_BF16)
        kb_ref[:, hd * KB_LANES:hd * KB_LANES + LANES] = qk_norm(block(b_k0 + hd * LANES), gkb_ref[...]).astype(_BF16)
        kb_ref[:, hd * KB_LANES + LANES:(hd + 1) * KB_LANES] = pos_feat
        vtb_ref[hd, 0, 0:B_VDIM, :] = block(b_v0 + hd * LANES).T.astype(_BF16)
        vtb_ref[hd, 0, B_VDIM:B_VDIM + BF16_SUBLANES, :] = ones_rows


def _online_columns(w, k_ref, vt_ref, s_scr, p_scr, bias_fn):
    n = w.shape[1]
    rows = vt_ref.shape[1]
    nk = vt_ref.shape[0]
    assert nk % 2 == 0

    def scores(j, slot):
        k = k_ref[pl.ds(pl.multiple_of(j * KV_CHUNK, KV_CHUNK), KV_CHUNK), :]
        s = _dot(k, w)
        if bias_fn is not None:
            s = s - bias_fn(j)
        s_scr[slot] = s
        return jnp.max(s, axis=0, keepdims=True)

    def weights(slot, m, cmax):
        m_new = jnp.maximum(m, cmax)
        p_scr[slot] = jnp.exp(s_scr[slot] - m_new).astype(_BF16)
        return m_new, jnp.exp(m - m_new)

    def values(j, slot, acc, alpha):
        return alpha * acc + _dot(vt_ref[j], p_scr[slot])

    m = jnp.full((1, n), NEG_BIG, _F32)
    acc = jnp.zeros((rows, n), _F32)
    cmax0 = scores(0, 0)
    cmax1 = scores(1, 1)
    m, alpha = weights(0, m, cmax0)

    def body(i, carry):
        m, alpha, cmax1, acc = carry
        j = 2 * i + 1
        cmax0 = scores(j + 1, 0)
        acc = values(j - 1, 0, acc, alpha)
        m, alpha = weights(1, m, cmax1)
        cmax1 = scores(j + 2, 1)
        acc = values(j, 1, acc, alpha)
        m, alpha = weights(0, m, cmax0)
        return m, alpha, cmax1, acc

    m, alpha, cmax1, acc = lax.fori_loop(0, nk // 2 - 1, body, (m, alpha, cmax1, acc))
    acc = values(nk - 2, 0, acc, alpha)
    m, alpha = weights(1, m, cmax1)
    return values(nk - 1, 1, acc, alpha)


def _sublane_partial_sums(p):
    return jnp.sum(p.reshape(p.shape[0] // 8, 8, p.shape[1]), axis=0)


def _shifted_columns(w_fns, k_ref, vt_ref, s_scr, accs, denoms, finish):
    nk = vt_ref.shape[0]
    steps = [(t, c) for t in range(len(w_fns)) for c in range(nk)]

    def scores(t, c, slot):
        k = k_ref[pl.ds(c * KV_CHUNK, KV_CHUNK), :]
        s_scr[slot] = _dot(k, w_fns[t](c))

    scores(*steps[0], 0)
    for i, (t, c) in enumerate(steps):
        if i + 1 < len(steps):
            scores(*steps[i + 1], (i + 1) % 2)
        p = jnp.exp(s_scr[i % 2])
        if denoms[t] is not None:
            denoms[t] = denoms[t] + _sublane_partial_sums(p)
        accs[t] = accs[t] + _dot(vt_ref[c, 0:accs[t].shape[0], :], p.astype(_BF16))
        if c == nk - 1:
            finish(t, accs[t], denoms[t])


def _feature_rows(rows_list, n):
    row = lax.broadcasted_iota(jnp.int32, (BF16_SUBLANES, n), 0)
    tile = jnp.zeros((BF16_SUBLANES, n), _F32)
    for r, v in enumerate(rows_list):
        tile = jnp.where(row == r, v, tile)
    return tile.astype(_BF16)


def _tile_cols(t, tq):
    return slice(t * tq, (t + 1) * tq)


def _a_queries(qt_ref, cols):
    return jnp.concatenate([qt_ref[r * HEAD_DIM:(r + 1) * HEAD_DIM, cols] for r in range(A_GROUP)], axis=1)


def _a_finish(acc, o_ref, cols):
    tq = cols.stop - cols.start
    out = acc[0:HEAD_DIM, :] / acc[HEAD_DIM:HEAD_DIM + 1, :]
    for r in range(A_GROUP):
        o_ref[r * HEAD_DIM:(r + 1) * HEAD_DIM, cols] = out[:, r * tq:(r + 1) * tq].astype(o_ref.dtype)


def _attn_a_shift_body(qt_ref, k_ref, vt_ref, bound_ref, o_ref, s_scr):
    n = A_GROUP * A_Q_TOKENS
    feat = _feature_rows([-jnp.broadcast_to(bound_ref[...], (1, n))], n)
    pad = jnp.zeros((KA_LANES - HEAD_DIM - BF16_SUBLANES, n), _BF16)
    ws = [jnp.concatenate([_a_queries(qt_ref, _tile_cols(t, A_Q_TOKENS)), feat, pad], axis=0)
          for t in range(TILES_PER_STEP)]
    _shifted_columns([lambda c, w=w: w for w in ws], k_ref, vt_ref, s_scr,
                     [jnp.zeros((vt_ref.shape[1], n), _F32)] * TILES_PER_STEP, [None] * TILES_PER_STEP,
                     lambda t, acc, _: _a_finish(acc, o_ref, _tile_cols(t, A_Q_TOKENS)))


def _attn_a_online_body(qt_ref, k_ref, vt_ref, o_ref, s_scr, p_scr):
    for t in range(TILES_PER_STEP):
        wq = _a_queries(qt_ref, _tile_cols(t, A_Q_TOKENS))
        w = jnp.concatenate([wq, jnp.zeros((KA_LANES - HEAD_DIM, wq.shape[1]), _BF16)], axis=0)
        _a_finish(_online_columns(w, k_ref, vt_ref, s_scr, p_scr, None), o_ref, _tile_cols(t, A_Q_TOKENS))


def _b_queries(qt_ref, cols):
    zeros = jnp.zeros((HEAD_DIM, cols.stop - cols.start), _BF16)
    return jnp.concatenate([jnp.concatenate([qt_ref[0:HEAD_DIM, cols], zeros], axis=1),
                            jnp.concatenate([zeros, qt_ref[HEAD_DIM:2 * HEAD_DIM, cols]], axis=1)], axis=0)


def _b_tile_start(t):
    return (pl.program_id(1) * TILES_PER_STEP + t) * B_Q_TOKENS


def _b_finish(acc, denom, lq1_ref, lk1_ref, lq2_ref, lk2_ref, subln_ref, o_ref, cols, lambda_init):
    tq = cols.stop - cols.start
    lam = (jnp.exp(jnp.sum(lq1_ref[...] * lk1_ref[...], axis=-1, keepdims=True))
           - jnp.exp(jnp.sum(lq2_ref[...] * lk2_ref[...], axis=-1, keepdims=True)) + lambda_init)
    o0 = acc[0:B_VDIM, 0:tq] / denom[:, 0:tq]
    o1 = acc[0:B_VDIM, tq:2 * tq] / denom[:, tq:2 * tq]
    o = o0 - lam * o1
    o = o * lax.rsqrt(jnp.mean(o * o, axis=0, keepdims=True) + EPS) * subln_ref[...] * (1.0 - lambda_init)
    o_ref[:, cols] = o.astype(o_ref.dtype)


def _attn_b_shift_body(qt_ref, k_ref, vt_ref, bound_ref, slope_ref, lq1_ref, lk1_ref, lq2_ref, lk2_ref,
                       subln_ref, o_ref, w_scr, s_scr, *, lambda_init):
    tq = B_Q_TOKENS
    n = 2 * tq
    bound = jnp.broadcast_to(bound_ref[...], (1, n))
    slope = jnp.broadcast_to(slope_ref[...], (1, n))
    pad = jnp.zeros((KB_LANES - 2 * HEAD_DIM - BF16_SUBLANES, n), _BF16)
    zero = jnp.zeros((1, n), _F32)
    w_fns, accs, denoms = [], [], []
    for t in range(TILES_PER_STEP):
        wq = _b_queries(qt_ref, _tile_cols(t, tq))
        q_pos = _b_tile_start(t) + lax.broadcasted_iota(jnp.int32, (1, tq), 1)
        q_hi = (q_pos // POS_SPLIT * POS_SPLIT).astype(_F32)
        q_lo = (q_pos % POS_SPLIT).astype(_F32)
        sq_hi = slope * jnp.concatenate([q_hi, q_hi], axis=1)
        sq_lo = slope * jnp.concatenate([q_lo, q_lo], axis=1)
        before, after, dead = 3 * t, 3 * t + 1, 3 * t + 2
        w_scr[before] = jnp.concatenate([wq, _feature_rows([slope, slope, -sq_hi, -sq_lo, -bound], n), pad], axis=0)
        w_scr[after] = jnp.concatenate([wq, _feature_rows([-slope, -slope, sq_hi, sq_lo, -bound], n), pad], axis=0)
        w_scr[dead] = jnp.concatenate([wq, _feature_rows([zero, zero, zero, zero, -bound - DEAD_SHIFT], n), pad],
                                      axis=0)
        diag = _b_tile_start(t) // KV_CHUNK
        k_diag = k_ref[pl.ds(pl.multiple_of(diag * KV_CHUNK, KV_CHUNK), KV_CHUNK), :]
        p = jnp.exp(jnp.minimum(_dot(k_diag, w_scr[before]), _dot(k_diag, w_scr[after])))
        accs.append(_dot(vt_ref[diag, 0:B_VDIM, :], p.astype(_BF16)))
        denoms.append(_sublane_partial_sums(p))
        w_fns.append(lambda c, diag=diag, before=before:
                     w_scr[before + jnp.where(c < diag, 0, jnp.where(c > diag, 1, 2))])

    def finish(t, acc, denom):
        _b_finish(acc, jnp.sum(denom, axis=0, keepdims=True), lq1_ref, lk1_ref, lq2_ref, lk2_ref, subln_ref, o_ref,
                  _tile_cols(t, tq), lambda_init)

    _shifted_columns(w_fns, k_ref, vt_ref, s_scr, accs, denoms, finish)


def _attn_b_online_body(qt_ref, k_ref, vt_ref, slope_ref, lq1_ref, lk1_ref, lq2_ref, lk2_ref, subln_ref, o_ref,
                        s_scr, p_scr, *, lambda_init):
    tq = B_Q_TOKENS
    slope = slope_ref[...]
    k_off = lax.broadcasted_iota(jnp.int32, (KV_CHUNK, tq), 0)
    for t in range(TILES_PER_STEP):
        wq = _b_queries(qt_ref, _tile_cols(t, tq))
        w = jnp.concatenate([wq, jnp.zeros((KB_LANES - 2 * HEAD_DIM, 2 * tq), _BF16)], axis=0)
        q_pos = _b_tile_start(t) + lax.broadcasted_iota(jnp.int32, (KV_CHUNK, tq), 1)

        def bias_fn(j, q_pos=q_pos):
            dist = jnp.abs(q_pos - (k_off + j * KV_CHUNK)).astype(_F32)
            bias = slope * dist
            return jnp.concatenate([bias, bias], axis=1)

        acc = _online_columns(w, k_ref, vt_ref, s_scr, p_scr, bias_fn)
        _b_finish(acc, acc[B_VDIM:B_VDIM + 1, :], lq1_ref, lk1_ref, lq2_ref, lk2_ref, subln_ref, o_ref,
                  _tile_cols(t, tq), lambda_init)


def _out_body(x_ref, at_ref, bt_ref, w_out_ref, gain_ref, wg_ref, wu_ref, wd_ref, o_ref):
    tn = (((0,), (0,)), ((), ()))
    half = at_ref.shape[0]
    y = (lax.dot_general(at_ref[...], w_out_ref[0:half, :], tn, preferred_element_type=_F32)
         + lax.dot_general(bt_ref[...], w_out_ref[half:2 * half, :], tn, preferred_element_type=_F32))
    x = x_ref[...] + y
    o_ref[...] = _swiglu_half_step(x, gain_ref[...], wg_ref, wu_ref, wd_ref)


def _resident(shape):
    return pl.BlockSpec(shape, lambda *_: (0,) * len(shape), pipeline_mode=pl.Buffered(1))


def _score_scratch(n):
    return pltpu.VMEM((2, KV_CHUNK, n), _F32)


def _online_scratch(n):
    return [_score_scratch(n), pltpu.VMEM((2, KV_CHUNK, n), _BF16)]


def _params(n_axes):
    return pltpu.CompilerParams(dimension_semantics=("arbitrary",) * n_axes, vmem_limit_bytes=VMEM_LIMIT)


def _ffn_call(x2d, gain, wg, wu, wd):
    s = x2d.shape[0]
    row = pl.BlockSpec((ROW_TILE, D_MODEL), lambda i: (i, 0))
    return pl.pallas_call(
        _ffn_body, name="ffn_half_step",
        grid=(s // ROW_TILE,),
        in_specs=[row, _resident((1, D_MODEL)), _resident(wg.shape), _resident(wu.shape), _resident(wd.shape)],
        out_specs=row,
        out_shape=jax.ShapeDtypeStruct((s, D_MODEL), _F32),
        compiler_params=_params(1),
    )(x2d, gain, wg, wu, wd)


def _prep_call(x2d, gain, w_in, cos, sin, gqa, gka, gqb, gkb, seg):
    s = x2d.shape[0]
    nk = s // KV_CHUNK
    assert ROW_TILE == KV_CHUNK
    row = lambda width: pl.BlockSpec((ROW_TILE, width), lambda i: (i, 0))
    col = lambda height: pl.BlockSpec((height, ROW_TILE), lambda i: (0, i))
    vec = _resident((1, LANES))
    out_shape = (
        jax.ShapeDtypeStruct((A_Q_COLS, s), _BF16),
        jax.ShapeDtypeStruct((A_KV_HEADS, s, KA_LANES), _BF16),
        jax.ShapeDtypeStruct((A_KV_HEADS, nk, HEAD_DIM + BF16_SUBLANES, KV_CHUNK), _BF16),
        jax.ShapeDtypeStruct((B_QK_COLS, s), _BF16),
        jax.ShapeDtypeStruct((s, B_HEADS * KB_LANES), _BF16),
        jax.ShapeDtypeStruct((B_HEADS, nk, B_VDIM + BF16_SUBLANES, KV_CHUNK), _BF16),
    )
    out_specs = (
        col(A_Q_COLS),
        pl.BlockSpec((A_KV_HEADS, ROW_TILE, KA_LANES), lambda i: (0, i, 0)),
        pl.BlockSpec((A_KV_HEADS, 1, HEAD_DIM + BF16_SUBLANES, KV_CHUNK), lambda i: (0, i, 0, 0)),
        col(B_QK_COLS),
        row(B_HEADS * KB_LANES),
        pl.BlockSpec((B_HEADS, 1, B_VDIM + BF16_SUBLANES, KV_CHUNK), lambda i: (0, i, 0, 0)),
    )
    return pl.pallas_call(
        _prep_body, name="in_proj_prep",
        grid=(s // ROW_TILE,),
        in_specs=[row(D_MODEL), _resident((1, D_MODEL)), _resident(w_in.shape), row(LANES), row(LANES),
                  vec, vec, vec, vec, _resident((LANES, LANES))],
        out_specs=out_specs,
        out_shape=out_shape,
        compiler_params=_params(1),
    )(x2d, gain, w_in, cos, sin, gqa, gka, gqb, gkb, seg)


def _attn_a_call(qta, ka, vta, bound, shift_path):
    s = qta.shape[1]
    nk = s // KV_CHUNK
    rows = A_GROUP * HEAD_DIM
    step_tokens = TILES_PER_STEP * A_Q_TOKENS
    in_specs = [pl.BlockSpec((rows, step_tokens), lambda g, i: (g, i)),
                pl.BlockSpec((None, s, KA_LANES), lambda g, i: (g, 0, 0)),
                pl.BlockSpec((None, nk, HEAD_DIM + BF16_SUBLANES, KV_CHUNK), lambda g, i: (g, 0, 0, 0))]
    common = dict(
        grid=(A_KV_HEADS, s // step_tokens),
        out_specs=pl.BlockSpec((rows, step_tokens), lambda g, i: (g, i)),
        out_shape=jax.ShapeDtypeStruct((A_Q_COLS, s), _BF16),
        compiler_params=_params(2))
    if shift_path:
        return pl.pallas_call(_attn_a_shift_body, name="attn_gqa_shift",
                              in_specs=in_specs + [pl.BlockSpec((1, 1), lambda g, i: (0, 0))],
                              scratch_shapes=[_score_scratch(A_GROUP * A_Q_TOKENS)], **common)(qta, ka, vta, bound)
    return pl.pallas_call(_attn_a_online_body, name="attn_gqa_online", in_specs=in_specs,
                          scratch_shapes=_online_scratch(A_GROUP * A_Q_TOKENS), **common)(qta, ka, vta)


def _attn_b_call(qtb, kb, vtb, bound, slopes, lq1, lk1, lq2, lk2, subln, lambda_init, shift_path):
    s = qtb.shape[1]
    nk = s // KV_CHUNK
    n = 2 * B_Q_TOKENS
    per_head = lambda width: pl.BlockSpec((None, 1, width), lambda h, i: (h, 0, 0))
    step_tokens = TILES_PER_STEP * B_Q_TOKENS
    qkv_specs = [pl.BlockSpec((2 * HEAD_DIM, step_tokens), lambda h, i: (h, i)),
                 pl.BlockSpec((s, KB_LANES), lambda h, i: (0, h)),
                 pl.BlockSpec((None, nk, B_VDIM + BF16_SUBLANES, KV_CHUNK), lambda h, i: (h, 0, 0, 0))]
    param_specs = [per_head(1), per_head(HEAD_DIM), per_head(HEAD_DIM), per_head(HEAD_DIM), per_head(HEAD_DIM),
                   pl.BlockSpec((B_VDIM, 1), lambda h, i: (0, 0))]
    common = dict(
        grid=(B_HEADS, s // step_tokens),
        out_specs=pl.BlockSpec((B_VDIM, step_tokens), lambda h, i: (h, i)),
        out_shape=jax.ShapeDtypeStruct((B_V_COLS, s), _BF16),
        compiler_params=_params(2))
    params = (slopes, lq1, lk1, lq2, lk2, subln)
    if shift_path:
        return pl.pallas_call(
            functools.partial(_attn_b_shift_body, lambda_init=lambda_init), name="attn_diff_shift",
            in_specs=qkv_specs + [pl.BlockSpec((1, 1), lambda h, i: (0, 0))] + param_specs,
            scratch_shapes=[pltpu.VMEM((3 * TILES_PER_STEP, KB_LANES, n), _BF16), _score_scratch(n)],
            **common)(qtb, kb, vtb, bound, *params)
    return pl.pallas_call(
        functools.partial(_attn_b_online_body, lambda_init=lambda_init), name="attn_diff_online",
        in_specs=qkv_specs + param_specs, scratch_shapes=_online_scratch(n), **common)(qtb, kb, vtb, *params)


def _out_call(x2d, at, bt, w_out, gain, wg, wu, wd):
    s = x2d.shape[0]
    row = pl.BlockSpec((ROW_TILE, D_MODEL), lambda i: (i, 0))
    col = pl.BlockSpec((at.shape[0], ROW_TILE), lambda i: (0, i))
    return pl.pallas_call(
        _out_body, name="out_proj_ffn",
        grid=(s // ROW_TILE,),
        in_specs=[row, col, col, _resident(w_out.shape), _resident((1, D_MODEL)),
                  _resident(wg.shape), _resident(wu.shape), _resident(wd.shape)],
        out_specs=row,
        out_shape=jax.ShapeDtypeStruct((s, D_MODEL), _F32),
        compiler_params=_params(1),
    )(x2d, at, bt, w_out, gain, wg, wu, wd)


def _rope_tables(s):
    rows = s // GRID_W
    row = jnp.broadcast_to(jnp.arange(rows, dtype=_F32)[:, None], (rows, GRID_W)).reshape(-1)
    colp = jnp.broadcast_to(jnp.arange(GRID_W, dtype=_F32)[None, :], (rows, GRID_W)).reshape(-1)
    inv_freq = ROPE_THETA ** (-jnp.arange(0, AX_DIM, 2, dtype=_F32) / AX_DIM)
    ang_r = row[:, None] * inv_freq[None, :]
    ang_c = colp[:, None] * inv_freq[None, :]
    cos = jnp.concatenate([jnp.cos(ang_r)] * 2 + [jnp.cos(ang_c)] * 2, axis=-1)
    sin = jnp.concatenate([-jnp.sin(ang_r), jnp.sin(ang_r), -jnp.sin(ang_c), jnp.sin(ang_c)], axis=-1)
    return jnp.tile(cos, (1, LANES // HEAD_DIM)), jnp.tile(sin, (1, LANES // HEAD_DIM))


def _score_bound(q_gain, k_gain):
    bound = math.sqrt(HEAD_DIM) * jnp.max(jnp.abs(q_gain)) * jnp.max(jnp.abs(k_gain)) * BOUND_MARGIN
    return bound.astype(_F32).reshape(1, 1)


def kernel(x, ffn1_norm, ffn1_w_gate, ffn1_w_up, ffn1_w_down, attn_norm, w_in, a_q_norm, a_k_norm, b_q_norm, b_k_norm, lambda_q1, lambda_k1, lambda_q2, lambda_k2, b_subln, w_out, ffn2_norm, ffn2_w_gate, ffn2_w_up, ffn2_w_down):
    bsz, s, _ = x.shape
    depth = w_in.shape[0]
    assert s % ROW_TILE == 0 and s % KV_CHUNK == 0 and s % (TILES_PER_STEP * B_Q_TOKENS) == 0
    assert KV_CHUNK % B_Q_TOKENS == 0 and s < POS_SPLIT * 256
    cos, sin = _rope_tables(s)
    lane_chunk = jnp.arange(LANES) // HEAD_DIM
    seg = ((lane_chunk[:, None] == lane_chunk[None, :]).astype(_F32) / HEAD_DIM).astype(_BF16)
    slopes = (2.0 ** (-8.0 * (jnp.arange(B_HEADS, dtype=_F32) + 1.0) / B_HEADS)).reshape(B_HEADS, 1, 1)
    lane_gain = lambda g: jnp.tile(g, LANES // HEAD_DIM).reshape(1, LANES)
    bf = lambda w: w.astype(_BF16)

    outs = []
    for b in range(bsz):
        xb = x[b]
        for l in range(depth):
            lambda_init = 0.8 - 0.6 * math.exp(-0.3 * l)
            xb = _ffn_call(xb, ffn1_norm[l].reshape(1, D_MODEL), bf(ffn1_w_gate[l]), bf(ffn1_w_up[l]), bf(ffn1_w_down[l]))
            qta, ka, vta, qtb, kb, vtb = _prep_call(
                xb, attn_norm[l].reshape(1, D_MODEL), bf(w_in[l]), cos, sin,
                lane_gain(a_q_norm[l]), lane_gain(a_k_norm[l]), lane_gain(b_q_norm[l]), lane_gain(b_k_norm[l]), seg)
            per_head = lambda p: p[l].reshape(B_HEADS, 1, HEAD_DIM)
            b_params = (slopes, per_head(lambda_q1), per_head(lambda_k1), per_head(lambda_q2), per_head(lambda_k2),
                        b_subln[l].reshape(B_VDIM, 1), lambda_init)

            bound_a = _score_bound(a_q_norm[l], a_k_norm[l])
            bound_b = _score_bound(b_q_norm[l], b_k_norm[l])

            def mixers(shift_path, qta=qta, ka=ka, vta=vta, qtb=qtb, kb=kb, vtb=vtb, b_params=b_params,
                       bound_a=bound_a, bound_b=bound_b):
                return (_attn_a_call(qta, ka, vta, bound_a, shift_path),
                        _attn_b_call(qtb, kb, vtb, bound_b, *b_params, shift_path))

            at, bt = lax.cond(jnp.maximum(bound_a, bound_b)[0, 0] <= SHIFT_PATH_MAX_BOUND,
                              lambda: mixers(True), lambda: mixers(False))
            xb = _out_call(xb, at, bt, bf(w_out[l]), ffn2_norm[l].reshape(1, D_MODEL),
                           bf(ffn2_w_gate[l]), bf(ffn2_w_up[l]), bf(ffn2_w_down[l]))
        outs.append(xb)
    return jnp.stack(outs, axis=0)
```

```python
import functools
import math

import jax
import jax.numpy as jnp
from jax import lax
from jax.experimental import pallas as pl
from jax.experimental.pallas import tpu as pltpu

D_MODEL = 1024
HEAD_DIM = 64
GRID_W = 64
EPS = 1e-6
ROPE_THETA = 10000.0
AX_DIM = HEAD_DIM // 2
A_HEADS = 8
A_KV_HEADS = 2
A_GROUP = A_HEADS // A_KV_HEADS
B_HEADS = 4
B_VDIM = 2 * HEAD_DIM
A_Q_COLS = A_HEADS * HEAD_DIM
A_KV_COLS = A_KV_HEADS * HEAD_DIM
B_QK_COLS = B_HEADS * 2 * HEAD_DIM
B_V_COLS = B_HEADS * B_VDIM
IN_COLS = A_Q_COLS + 2 * A_KV_COLS + 2 * B_QK_COLS + B_V_COLS
D_FF = 2816

LANES = 128
BF16_SUBLANES = 16
ROW_TILE = 512
KV_CHUNK = 512
A_Q_TOKENS = 128
B_Q_TOKENS = 256
TILES_PER_STEP = 2
FF_SPLITS = (0, 1536, D_FF)
VMEM_LIMIT = 56 * 1024 * 1024
NEG_BIG = -1e30

KA_LANES = 2 * HEAD_DIM
KB_LANES = 4 * HEAD_DIM
POS_SPLIT = 128
BOUND_MARGIN = 1.0 + 2.0 ** -6
SHIFT_PATH_MAX_BOUND = 40.0
DEAD_SHIFT = 256.0

_F32 = jnp.float32
_BF16 = jnp.bfloat16


def _dot(a, b):
    return jnp.dot(a, b, preferred_element_type=_F32)


def _rms_rows(x, gain):
    return x * lax.rsqrt(jnp.mean(x * x, axis=-1, keepdims=True) + EPS) * gain


def _swiglu_half_step(x, gain, wg_ref, wu_ref, wd_ref):
    h = _rms_rows(x, gain).astype(_BF16)
    acc = jnp.zeros(x.shape, _F32)
    for lo, hi in zip(FF_SPLITS[:-1], FF_SPLITS[1:]):
        cols = slice(lo, hi)
        g = _dot(h, wg_ref[:, cols])
        u = _dot(h, wu_ref[:, cols])
        a = (g / (1.0 + jnp.exp(-g)) * u).astype(_BF16)
        acc = acc + _dot(a, wd_ref[cols, :])
    return x + 0.5 * acc


def _ffn_body(x_ref, gain_ref, wg_ref, wu_ref, wd_ref, o_ref):
    o_ref[...] = _swiglu_half_step(x_ref[...], gain_ref[...], wg_ref, wu_ref, wd_ref)


def _chunk_sums(x, seg):
    hi = x.astype(_BF16)
    lo = (x - hi.astype(_F32)).astype(_BF16)
    return _dot(jnp.concatenate([hi, lo], axis=1), seg)


def _prep_body(x_ref, gain_ref, w_in_ref, cos_ref, sin_ref, gqa_ref, gka_ref, gqb_ref, gkb_ref, seg_ref,
               qta_ref, ka_ref, vta_ref, qtb_ref, kb_ref, vtb_ref):
    rows = x_ref.shape[0]
    h = _rms_rows(x_ref[...], gain_ref[...]).astype(_BF16)
    proj = _dot(h, w_in_ref[...])
    seg = seg_ref[...]
    scale = 1.0 / math.sqrt(HEAD_DIM)

    def block(col):
        return proj[:, col:col + LANES]

    def qk_norm(blk, gain):
        return blk * lax.rsqrt(_chunk_sums(blk * blk, seg) + EPS) * gain

    lane = lax.broadcasted_iota(jnp.int32, (rows, LANES), 1)
    first_half = (lane % AX_DIM) < (AX_DIM // 2)
    cos = cos_ref[...]
    sin = sin_ref[...]

    def rope(blk):
        partner = jnp.where(first_half, pltpu.roll(blk, LANES - AX_DIM // 2, 1), pltpu.roll(blk, AX_DIM // 2, 1))
        return blk * cos + partner * sin

    for b in range(A_Q_COLS // LANES):
        q = rope(qk_norm(block(b * LANES), gqa_ref[...])) * scale
        qta_ref[b * LANES:(b + 1) * LANES, :] = q.T.astype(_BF16)
    k = rope(qk_norm(block(A_Q_COLS), gka_ref[...])).astype(_BF16)
    vt = block(A_Q_COLS + A_KV_COLS).T.astype(_BF16)
    ones_rows = jnp.ones((BF16_SUBLANES, rows), _BF16)
    one_lane = (lax.broadcasted_iota(jnp.int32, (rows, HEAD_DIM), 1) == 0).astype(_BF16)
    for g in range(A_KV_HEADS):
        ka_ref[g] = jnp.concatenate([k[:, g * HEAD_DIM:(g + 1) * HEAD_DIM], one_lane], axis=1)
        vta_ref[g, 0, 0:HEAD_DIM, :] = vt[g * HEAD_DIM:(g + 1) * HEAD_DIM, :]
        vta_ref[g, 0, HEAD_DIM:HEAD_DIM + BF16_SUBLANES, :] = ones_rows
    pos = pl.program_id(0) * rows + lax.broadcasted_iota(jnp.int32, (rows, LANES), 0)
    pos_feat = jnp.where(lane == 0, (pos // POS_SPLIT * POS_SPLIT).astype(_F32),
                         jnp.where(lane == 1, (pos % POS_SPLIT).astype(_F32),
                                   jnp.where(lane < 5, 1.0, 0.0))).astype(_BF16)
    b_q0 = A_Q_COLS + 2 * A_KV_COLS
    b_k0 = b_q0 + B_QK_COLS
    b_v0 = b_k0 + B_QK_COLS
    for hd in range(B_HEADS):
        q = qk_norm(block(b_q0 + hd * LANES), gqb_ref[...]) * scale
        qtb_ref[hd * LANES:(hd + 1) * LANES, :] = q.T.astype(_BF16)
        kb_ref[:, hd * KB_LANES:hd * KB_LANES + LANES] = qk_norm(block(b_k0 + hd * LANES), gkb_ref[...]).astype(_BF16)
        kb_ref[:, hd * KB_LANES + LANES:(hd + 1) * KB_LANES] = pos_feat
        vtb_ref[hd, 0, 0:B_VDIM, :] = block(b_v0 + hd * LANES).T.astype(_BF16)
        vtb_ref[hd, 0, B_VDIM:B_VDIM + BF16_SUBLANES, :] = ones_rows


def _online_columns(w, k_ref, vt_ref, s_scr, p_scr, bias_fn):
    n = w.shape[1]
    rows = vt_ref.shape[1]
    nk = vt_ref.shape[0]
    assert nk % 2 == 0

    def scores(j, slot):
        k = k_ref[pl.ds(pl.multiple_of(j * KV_CHUNK, KV_CHUNK), KV_CHUNK), :]
        s = _dot(k, w)
        if bias_fn is not None:
            s = s - bias_fn(j)
        s_scr[slot] = s
        return jnp.max(s, axis=0, keepdims=True)

    def weights(slot, m, cmax):
        m_new = jnp.maximum(m, cmax)
        p_scr[slot] = jnp.exp(s_scr[slot] - m_new).astype(_BF16)
        return m_new, jnp.exp(m - m_new)

    def values(j, slot, acc, alpha):
        return alpha * acc + _dot(vt_ref[j], p_scr[slot])

    m = jnp.full((1, n), NEG_BIG, _F32)
    acc = jnp.zeros((rows, n), _F32)
    cmax0 = scores(0, 0)
    cmax1 = scores(1, 1)
    m, alpha = weights(0, m, cmax0)

    def body(i, carry):
        m, alpha, cmax1, acc = carry
        j = 2 * i + 1
        cmax0 = scores(j + 1, 0)
        acc = values(j - 1, 0, acc, alpha)
        m, alpha = weights(1, m, cmax1)
        cmax1 = scores(j + 2, 1)
        acc = values(j, 1, acc, alpha)
        m, alpha = weights(0, m, cmax0)
        return m, alpha, cmax1, acc

    m, alpha, cmax1, acc = lax.fori_loop(0, nk // 2 - 1, body, (m, alpha, cmax1, acc))
    acc = values(nk - 2, 0, acc, alpha)
    m, alpha = weights(1, m, cmax1)
    return values(nk - 1, 1, acc, alpha)


def _sublane_partial_sums(p):
    return jnp.sum(p.reshape(p.shape[0] // 8, 8, p.shape[1]), axis=0)


def _shifted_columns(w_fns, k_ref, vt_ref, s_scr, accs, denoms, finish):
    nk = vt_ref.shape[0]
    steps = [(t, c) for t in range(len(w_fns)) for c in range(nk)]
    halves = [slice(h * KV_CHUNK // 2, (h + 1) * KV_CHUNK // 2) for h in range(2)]

    def scores(t, c, slot, rows):
        k = k_ref[pl.ds(c * KV_CHUNK + rows.start, rows.stop - rows.start), :]
        s_scr[slot, rows, :] = _dot(k, w_fns[t](c))

    for rows in halves:
        scores(*steps[0], 0, rows)
    for i, (t, c) in enumerate(steps):
        for rows in halves:
            if i + 1 < len(steps):
                scores(*steps[i + 1], (i + 1) % 2, rows)
            p = jnp.exp(s_scr[i % 2, rows, :])
            if denoms[t] is not None:
                denoms[t] = denoms[t] + _sublane_partial_sums(p)
            accs[t] = accs[t] + _dot(vt_ref[c, 0:accs[t].shape[0], rows], p.astype(_BF16))
        if c == nk - 1:
            finish(t, accs[t], denoms[t])


def _feature_rows(rows_list, n):
    row = lax.broadcasted_iota(jnp.int32, (BF16_SUBLANES, n), 0)
    tile = jnp.zeros((BF16_SUBLANES, n), _F32)
    for r, v in enumerate(rows_list):
        tile = jnp.where(row == r, v, tile)
    return tile.astype(_BF16)


def _tile_cols(t, tq):
    return slice(t * tq, (t + 1) * tq)


def _a_queries(qt_ref, cols):
    return jnp.concatenate([qt_ref[r * HEAD_DIM:(r + 1) * HEAD_DIM, cols] for r in range(A_GROUP)], axis=1)


def _a_finish(acc, o_ref, cols):
    tq = cols.stop - cols.start
    out = acc[0:HEAD_DIM, :] / acc[HEAD_DIM:HEAD_DIM + 1, :]
    for r in range(A_GROUP):
        o_ref[r * HEAD_DIM:(r + 1) * HEAD_DIM, cols] = out[:, r * tq:(r + 1) * tq].astype(o_ref.dtype)


def _attn_a_shift_body(qt_ref, k_ref, vt_ref, bound_ref, o_ref, s_scr):
    n = A_GROUP * A_Q_TOKENS
    feat = _feature_rows([-jnp.broadcast_to(bound_ref[...], (1, n))], n)
    pad = jnp.zeros((KA_LANES - HEAD_DIM - BF16_SUBLANES, n), _BF16)
    ws = [jnp.concatenate([_a_queries(qt_ref, _tile_cols(t, A_Q_TOKENS)), feat, pad], axis=0)
          for t in range(TILES_PER_STEP)]
    _shifted_columns([lambda c, w=w: w for w in ws], k_ref, vt_ref, s_scr,
                     [jnp.zeros((vt_ref.shape[1], n), _F32)] * TILES_PER_STEP, [None] * TILES_PER_STEP,
                     lambda t, acc, _: _a_finish(acc, o_ref, _tile_cols(t, A_Q_TOKENS)))


def _attn_a_online_body(qt_ref, k_ref, vt_ref, o_ref, s_scr, p_scr):
    for t in range(TILES_PER_STEP):
        wq = _a_queries(qt_ref, _tile_cols(t, A_Q_TOKENS))
        w = jnp.concatenate([wq, jnp.zeros((KA_LANES - HEAD_DIM, wq.shape[1]), _BF16)], axis=0)
        _a_finish(_online_columns(w, k_ref, vt_ref, s_scr, p_scr, None), o_ref, _tile_cols(t, A_Q_TOKENS))


def _b_queries(qt_ref, cols):
    zeros = jnp.zeros((HEAD_DIM, cols.stop - cols.start), _BF16)
    return jnp.concatenate([jnp.concatenate([qt_ref[0:HEAD_DIM, cols], zeros], axis=1),
                            jnp.concatenate([zeros, qt_ref[HEAD_DIM:2 * HEAD_DIM, cols]], axis=1)], axis=0)


def _b_tile_start(t):
    return (pl.program_id(1) * TILES_PER_STEP + t) * B_Q_TOKENS


def _b_finish(acc, denom, lq1_ref, lk1_ref, lq2_ref, lk2_ref, subln_ref, o_ref, cols, lambda_init):
    tq = cols.stop - cols.start
    lam = (jnp.exp(jnp.sum(lq1_ref[...] * lk1_ref[...], axis=-1, keepdims=True))
           - jnp.exp(jnp.sum(lq2_ref[...] * lk2_ref[...], axis=-1, keepdims=True)) + lambda_init)
    o0 = acc[0:B_VDIM, 0:tq] / denom[:, 0:tq]
    o1 = acc[0:B_VDIM, tq:2 * tq] / denom[:, tq:2 * tq]
    o = o0 - lam * o1
    o = o * lax.rsqrt(jnp.mean(o * o, axis=0, keepdims=True) + EPS) * subln_ref[...] * (1.0 - lambda_init)
    o_ref[:, cols] = o.astype(o_ref.dtype)


def _attn_b_shift_body(qt_ref, k_ref, vt_ref, bound_ref, slope_ref, lq1_ref, lk1_ref, lq2_ref, lk2_ref,
                       subln_ref, o_ref, w_scr, s_scr, *, lambda_init):
    tq = B_Q_TOKENS
    n = 2 * tq
    bound = jnp.broadcast_to(bound_ref[...], (1, n))
    slope = jnp.broadcast_to(slope_ref[...], (1, n))
    pad = jnp.zeros((KB_LANES - 2 * HEAD_DIM - BF16_SUBLANES, n), _BF16)
    zero = jnp.zeros((1, n), _F32)
    w_fns, accs, denoms = [], [], []
    for t in range(TILES_PER_STEP):
        wq = _b_queries(qt_ref, _tile_cols(t, tq))
        q_pos = _b_tile_start(t) + lax.broadcasted_iota(jnp.int32, (1, tq), 1)
        q_hi = (q_pos // POS_SPLIT * POS_SPLIT).astype(_F32)
        q_lo = (q_pos % POS_SPLIT).astype(_F32)
        sq_hi = slope * jnp.concatenate([q_hi, q_hi], axis=1)
        sq_lo = slope * jnp.concatenate([q_lo, q_lo], axis=1)
        before, after, dead = 3 * t, 3 * t + 1, 3 * t + 2
        w_scr[before] = jnp.concatenate([wq, _feature_rows([slope, slope, -sq_hi, -sq_lo, -bound], n), pad], axis=0)
        w_scr[after] = jnp.concatenate([wq, _feature_rows([-slope, -slope, sq_hi, sq_lo, -bound], n), pad], axis=0)
        w_scr[dead] = jnp.concatenate([wq, _feature_rows([zero, zero, zero, zero, -bound - DEAD_SHIFT], n), pad],
                                      axis=0)
        diag = _b_tile_start(t) // KV_CHUNK
        k_diag = k_ref[pl.ds(pl.multiple_of(diag * KV_CHUNK, KV_CHUNK), KV_CHUNK), :]
        p = jnp.exp(jnp.minimum(_dot(k_diag, w_scr[before]), _dot(k_diag, w_scr[after])))
        accs.append(_dot(vt_ref[diag, 0:B_VDIM, :], p.astype(_BF16)))
        denoms.append(_sublane_partial_sums(p))
        w_fns.append(lambda c, diag=diag, before=before:
                     w_scr[before + jnp.where(c < diag, 0, jnp.where(c > diag, 1, 2))])

    def finish(t, acc, denom):
        _b_finish(acc, jnp.sum(denom, axis=0, keepdims=True), lq1_ref, lk1_ref, lq2_ref, lk2_ref, subln_ref, o_ref,
                  _tile_cols(t, tq), lambda_init)

    _shifted_columns(w_fns, k_ref, vt_ref, s_scr, accs, denoms, finish)


def _attn_b_online_body(qt_ref, k_ref, vt_ref, slope_ref, lq1_ref, lk1_ref, lq2_ref, lk2_ref, subln_ref, o_ref,
                        s_scr, p_scr, *, lambda_init):
    tq = B_Q_TOKENS
    slope = slope_ref[...]
    k_off = lax.broadcasted_iota(jnp.int32, (KV_CHUNK, tq), 0)
    for t in range(TILES_PER_STEP):
        wq = _b_queries(qt_ref, _tile_cols(t, tq))
        w = jnp.concatenate([wq, jnp.zeros((KB_LANES - 2 * HEAD_DIM, 2 * tq), _BF16)], axis=0)
        q_pos = _b_tile_start(t) + lax.broadcasted_iota(jnp.int32, (KV_CHUNK, tq), 1)

        def bias_fn(j, q_pos=q_pos):
            dist = jnp.abs(q_pos - (k_off + j * KV_CHUNK)).astype(_F32)
            bias = slope * dist
            return jnp.concatenate([bias, bias], axis=1)

        acc = _online_columns(w, k_ref, vt_ref, s_scr, p_scr, bias_fn)
        _b_finish(acc, acc[B_VDIM:B_VDIM + 1, :], lq1_ref, lk1_ref, lq2_ref, lk2_ref, subln_ref, o_ref,
                  _tile_cols(t, tq), lambda_init)


def _out_body(x_ref, at_ref, bt_ref, w_out_ref, gain_ref, wg_ref, wu_ref, wd_ref, o_ref):
    tn = (((0,), (0,)), ((), ()))
    half = at_ref.shape[0]
    y = (lax.dot_general(at_ref[...], w_out_ref[0:half, :], tn, preferred_element_type=_F32)
         + lax.dot_general(bt_ref[...], w_out_ref[half:2 * half, :], tn, preferred_element_type=_F32))
    x = x_ref[...] + y
    o_ref[...] = _swiglu_half_step(x, gain_ref[...], wg_ref, wu_ref, wd_ref)


def _resident(shape):
    return pl.BlockSpec(shape, lambda *_: (0,) * len(shape), pipeline_mode=pl.Buffered(1))


def _score_scratch(n):
    return pltpu.VMEM((2, KV_CHUNK, n), _F32)


def _online_scratch(n):
    return [_score_scratch(n), pltpu.VMEM((2, KV_CHUNK, n), _BF16)]


def _params(n_axes):
    return pltpu.CompilerParams(dimension_semantics=("arbitrary",) * n_axes, vmem_limit_bytes=VMEM_LIMIT)


def _ffn_call(x2d, gain, wg, wu, wd):
    s = x2d.shape[0]
    row = pl.BlockSpec((ROW_TILE, D_MODEL), lambda i: (i, 0))
    return pl.pallas_call(
        _ffn_body, name="ffn_half_step",
        grid=(s // ROW_TILE,),
        in_specs=[row, _resident((1, D_MODEL)), _resident(wg.shape), _resident(wu.shape), _resident(wd.shape)],
        out_specs=row,
        out_shape=jax.ShapeDtypeStruct((s, D_MODEL), _F32),
        compiler_params=_params(1),
    )(x2d, gain, wg, wu, wd)


def _prep_call(x2d, gain, w_in, cos, sin, gqa, gka, gqb, gkb, seg):
    s = x2d.shape[0]
    nk = s // KV_CHUNK
    assert ROW_TILE == KV_CHUNK
    row = lambda width: pl.BlockSpec((ROW_TILE, width), lambda i: (i, 0))
    col = lambda height: pl.BlockSpec((height, ROW_TILE), lambda i: (0, i))
    vec = _resident((1, LANES))
    out_shape = (
        jax.ShapeDtypeStruct((A_Q_COLS, s), _BF16),
        jax.ShapeDtypeStruct((A_KV_HEADS, s, KA_LANES), _BF16),
        jax.ShapeDtypeStruct((A_KV_HEADS, nk, HEAD_DIM + BF16_SUBLANES, KV_CHUNK), _BF16),
        jax.ShapeDtypeStruct((B_QK_COLS, s), _BF16),
        jax.ShapeDtypeStruct((s, B_HEADS * KB_LANES), _BF16),
        jax.ShapeDtypeStruct((B_HEADS, nk, B_VDIM + BF16_SUBLANES, KV_CHUNK), _BF16),
    )
    out_specs = (
        col(A_Q_COLS),
        pl.BlockSpec((A_KV_HEADS, ROW_TILE, KA_LANES), lambda i: (0, i, 0)),
        pl.BlockSpec((A_KV_HEADS, 1, HEAD_DIM + BF16_SUBLANES, KV_CHUNK), lambda i: (0, i, 0, 0)),
        col(B_QK_COLS),
        row(B_HEADS * KB_LANES),
        pl.BlockSpec((B_HEADS, 1, B_VDIM + BF16_SUBLANES, KV_CHUNK), lambda i: (0, i, 0, 0)),
    )
    return pl.pallas_call(
        _prep_body, name="in_proj_prep",
        grid=(s // ROW_TILE,),
        in_specs=[row(D_MODEL), _resident((1, D_MODEL)), _resident(w_in.shape), row(LANES), row(LANES),
                  vec, vec, vec, vec, _resident((2 * LANES, LANES))],
        out_specs=out_specs,
        out_shape=out_shape,
        compiler_params=_params(1),
    )(x2d, gain, w_in, cos, sin, gqa, gka, gqb, gkb, seg)


def _attn_a_call(qta, ka, vta, bound, shift_path):
    s = qta.shape[1]
    nk = s // KV_CHUNK
    rows = A_GROUP * HEAD_DIM
    step_tokens = TILES_PER_STEP * A_Q_TOKENS
    in_specs = [pl.BlockSpec((rows, step_tokens), lambda g, i: (g, i)),
                pl.BlockSpec((None, s, KA_LANES), lambda g, i: (g, 0, 0)),
                pl.BlockSpec((None, nk, HEAD_DIM + BF16_SUBLANES, KV_CHUNK), lambda g, i: (g, 0, 0, 0))]
    common = dict(
        grid=(A_KV_HEADS, s // step_tokens),
        out_specs=pl.BlockSpec((rows, step_tokens), lambda g, i: (g, i)),
        out_shape=jax.ShapeDtypeStruct((A_Q_COLS, s), _BF16),
        compiler_params=_params(2))
    if shift_path:
        return pl.pallas_call(_attn_a_shift_body, name="attn_gqa_shift",
                              in_specs=in_specs + [pl.BlockSpec((1, 1), lambda g, i: (0, 0))],
                              scratch_shapes=[_score_scratch(A_GROUP * A_Q_TOKENS)], **common)(qta, ka, vta, bound)
    return pl.pallas_call(_attn_a_online_body, name="attn_gqa_online", in_specs=in_specs,
                          scratch_shapes=_online_scratch(A_GROUP * A_Q_TOKENS), **common)(qta, ka, vta)


def _attn_b_call(qtb, kb, vtb, bound, slopes, lq1, lk1, lq2, lk2, subln, lambda_init, shift_path):
    s = qtb.shape[1]
    nk = s // KV_CHUNK
    n = 2 * B_Q_TOKENS
    per_head = lambda width: pl.BlockSpec((None, 1, width), lambda h, i: (h, 0, 0))
    step_tokens = TILES_PER_STEP * B_Q_TOKENS
    qkv_specs = [pl.BlockSpec((2 * HEAD_DIM, step_tokens), lambda h, i: (h, i)),
                 pl.BlockSpec((s, KB_LANES), lambda h, i: (0, h)),
                 pl.BlockSpec((None, nk, B_VDIM + BF16_SUBLANES, KV_CHUNK), lambda h, i: (h, 0, 0, 0))]
    param_specs = [per_head(1), per_head(HEAD_DIM), per_head(HEAD_DIM), per_head(HEAD_DIM), per_head(HEAD_DIM),
                   pl.BlockSpec((B_VDIM, 1), lambda h, i: (0, 0))]
    common = dict(
        grid=(B_HEADS, s // step_tokens),
        out_specs=pl.BlockSpec((B_VDIM, step_tokens), lambda h, i: (h, i)),
        out_shape=jax.ShapeDtypeStruct((B_V_COLS, s), _BF16),
        compiler_params=_params(2))
    params = (slopes, lq1, lk1, lq2, lk2, subln)
    if shift_path:
        return pl.pallas_call(
            functools.partial(_attn_b_shift_body, lambda_init=lambda_init), name="attn_diff_shift",
            in_specs=qkv_specs + [pl.BlockSpec((1, 1), lambda h, i: (0, 0))] + param_specs,
            scratch_shapes=[pltpu.VMEM((3 * TILES_PER_STEP, KB_LANES, n), _BF16), _score_scratch(n)],
            **common)(qtb, kb, vtb, bound, *params)
    return pl.pallas_call(
        functools.partial(_attn_b_online_body, lambda_init=lambda_init), name="attn_diff_online",
        in_specs=qkv_specs + param_specs, scratch_shapes=_online_scratch(n), **common)(qtb, kb, vtb, *params)


def _out_call(x2d, at, bt, w_out, gain, wg, wu, wd):
    s = x2d.shape[0]
    row = pl.BlockSpec((ROW_TILE, D_MODEL), lambda i: (i, 0))
    col = pl.BlockSpec((at.shape[0], ROW_TILE), lambda i: (0, i))
    return pl.pallas_call(
        _out_body, name="out_proj_ffn",
        grid=(s // ROW_TILE,),
        in_specs=[row, col, col, _resident(w_out.shape), _resident((1, D_MODEL)),
                  _resident(wg.shape), _resident(wu.shape), _resident(wd.shape)],
        out_specs=row,
        out_shape=jax.ShapeDtypeStruct((s, D_MODEL), _F32),
        compiler_params=_params(1),
    )(x2d, at, bt, w_out, gain, wg, wu, wd)


def _rope_tables(s):
    rows = s // GRID_W
    row = jnp.broadcast_to(jnp.arange(rows, dtype=_F32)[:, None], (rows, GRID_W)).reshape(-1)
    colp = jnp.broadcast_to(jnp.arange(GRID_W, dtype=_F32)[None, :], (rows, GRID_W)).reshape(-1)
    inv_freq = ROPE_THETA ** (-jnp.arange(0, AX_DIM, 2, dtype=_F32) / AX_DIM)
    ang_r = row[:, None] * inv_freq[None, :]
    ang_c = colp[:, None] * inv_freq[None, :]
    cos = jnp.concatenate([jnp.cos(ang_r)] * 2 + [jnp.cos(ang_c)] * 2, axis=-1)
    sin = jnp.concatenate([-jnp.sin(ang_r), jnp.sin(ang_r), -jnp.sin(ang_c), jnp.sin(ang_c)], axis=-1)
    return jnp.tile(cos, (1, LANES // HEAD_DIM)), jnp.tile(sin, (1, LANES // HEAD_DIM))


def _score_bound(q_gain, k_gain):
    bound = math.sqrt(HEAD_DIM) * jnp.max(jnp.abs(q_gain)) * jnp.max(jnp.abs(k_gain)) * BOUND_MARGIN
    return bound.astype(_F32).reshape(1, 1)


def kernel(x, ffn1_norm, ffn1_w_gate, ffn1_w_up, ffn1_w_down, attn_norm, w_in, a_q_norm, a_k_norm, b_q_norm, b_k_norm, lambda_q1, lambda_k1, lambda_q2, lambda_k2, b_subln, w_out, ffn2_norm, ffn2_w_gate, ffn2_w_up, ffn2_w_down):
    bsz, s, _ = x.shape
    depth = w_in.shape[0]
    assert s % ROW_TILE == 0 and s % KV_CHUNK == 0 and s % (TILES_PER_STEP * B_Q_TOKENS) == 0
    assert KV_CHUNK % B_Q_TOKENS == 0 and s < POS_SPLIT * 256
    cos, sin = _rope_tables(s)
    lane_chunk = jnp.arange(LANES) // HEAD_DIM
    seg = ((lane_chunk[:, None] == lane_chunk[None, :]).astype(_F32) / HEAD_DIM).astype(_BF16)
    seg = jnp.concatenate([seg, seg], axis=0)
    slopes = (2.0 ** (-8.0 * (jnp.arange(B_HEADS, dtype=_F32) + 1.0) / B_HEADS)).reshape(B_HEADS, 1, 1)
    lane_gain = lambda g: jnp.tile(g, LANES // HEAD_DIM).reshape(1, LANES)
    bf = lambda w: w.astype(_BF16)

    outs = []
    for b in range(bsz):
        xb = x[b]
        for l in range(depth):
            lambda_init = 0.8 - 0.6 * math.exp(-0.3 * l)
            xb = _ffn_call(xb, ffn1_norm[l].reshape(1, D_MODEL), bf(ffn1_w_gate[l]), bf(ffn1_w_up[l]), bf(ffn1_w_down[l]))
            qta, ka, vta, qtb, kb, vtb = _prep_call(
                xb, attn_norm[l].reshape(1, D_MODEL), bf(w_in[l]), cos, sin,
                lane_gain(a_q_norm[l]), lane_gain(a_k_norm[l]), lane_gain(b_q_norm[l]), lane_gain(b_k_norm[l]), seg)
            per_head = lambda p: p[l].reshape(B_HEADS, 1, HEAD_DIM)
            b_params = (slopes, per_head(lambda_q1), per_head(lambda_k1), per_head(lambda_q2), per_head(lambda_k2),
                        b_subln[l].reshape(B_VDIM, 1), lambda_init)

            bound_a = _score_bound(a_q_norm[l], a_k_norm[l])
            bound_b = _score_bound(b_q_norm[l], b_k_norm[l])

            def mixers(shift_path, qta=qta, ka=ka, vta=vta, qtb=qtb, kb=kb, vtb=vtb, b_params=b_params,
                       bound_a=bound_a, bound_b=bound_b):
                return (_attn_a_call(qta, ka, vta, bound_a, shift_path),
                        _attn_b_call(qtb, kb, vtb, bound_b, *b_params, shift_path))

            at, bt = lax.cond(jnp.maximum(bound_a, bound_b)[0, 0] <= SHIFT_PATH_MAX_BOUND,
                              lambda: mixers(True), lambda: mixers(False))
            xb = _out_call(xb, at, bt, bf(w_out[l]), ffn2_norm[l].reshape(1, D_MODEL),
                           bf(ffn2_w_gate[l]), bf(ffn2_w_up[l]), bf(ffn2_w_down[l]))
        outs.append(xb)
    return jnp.stack(outs, axis=0)
```

```python
import functools
import math

import jax
import jax.numpy as jnp
from jax import lax
from jax.experimental import pallas as pl
from jax.experimental.pallas import tpu as pltpu

D_MODEL = 1024
HEAD_DIM = 64
GRID_W = 64
EPS = 1e-6
ROPE_THETA = 10000.0
AX_DIM = HEAD_DIM // 2
A_HEADS = 8
A_KV_HEADS = 2
A_GROUP = A_HEADS // A_KV_HEADS
B_HEADS = 4
B_VDIM = 2 * HEAD_DIM
A_Q_COLS = A_HEADS * HEAD_DIM
A_KV_COLS = A_KV_HEADS * HEAD_DIM
B_QK_COLS = B_HEADS * 2 * HEAD_DIM
B_V_COLS = B_HEADS * B_VDIM
IN_COLS = A_Q_COLS + 2 * A_KV_COLS + 2 * B_QK_COLS + B_V_COLS
D_FF = 2816

LANES = 128
BF16_SUBLANES = 16
ROW_TILE = 512
KV_CHUNK = 512
A_Q_TOKENS = 128
B_Q_TOKENS = 256
TILES_PER_STEP = 2
FF_SPLITS = (0, 1536, D_FF)
VMEM_LIMIT = 56 * 1024 * 1024
NEG_BIG = -1e30

KA_LANES = 2 * HEAD_DIM
KB_LANES = 4 * HEAD_DIM
POS_SPLIT = 128
BOUND_MARGIN = 1.0 + 2.0 ** -6
SHIFT_PATH_MAX_BOUND = 40.0

_F32 = jnp.float32
_BF16 = jnp.bfloat16


def _dot(a, b):
    return jnp.dot(a, b, preferred_element_type=_F32)


def _rms_rows(x, gain):
    return x * lax.rsqrt(jnp.mean(x * x, axis=-1, keepdims=True) + EPS) * gain


def _swiglu_half_step(x, gain, wg_ref, wu_ref, wd_ref):
    h = _rms_rows(x, gain).astype(_BF16)
    acc = jnp.zeros(x.shape, _F32)
    for lo, hi in zip(FF_SPLITS[:-1], FF_SPLITS[1:]):
        cols = slice(lo, hi)
        g = _dot(h, wg_ref[:, cols])
        u = _dot(h, wu_ref[:, cols])
        a = (g / (1.0 + jnp.exp(-g)) * u).astype(_BF16)
        acc = acc + _dot(a, wd_ref[cols, :])
    return x + 0.5 * acc


def _ffn_body(x_ref, gain_ref, wg_ref, wu_ref, wd_ref, o_ref):
    o_ref[...] = _swiglu_half_step(x_ref[...], gain_ref[...], wg_ref, wu_ref, wd_ref)


def _chunk_sums(x, seg):
    hi = x.astype(_BF16)
    lo = (x - hi.astype(_F32)).astype(_BF16)
    return _dot(jnp.concatenate([hi, lo], axis=1), seg)


def _prep_body(x_ref, gain_ref, w_in_ref, cos_ref, sin_ref, gqa_ref, gka_ref, gqb_ref, gkb_ref, seg_ref,
               qta_ref, ka_ref, vta_ref, qtb_ref, kb_ref, vtb_ref):
    rows = x_ref.shape[0]
    h = _rms_rows(x_ref[...], gain_ref[...]).astype(_BF16)
    proj = _dot(h, w_in_ref[...])
    seg = seg_ref[...]
    scale = 1.0 / math.sqrt(HEAD_DIM)

    def block(col):
        return proj[:, col:col + LANES]

    def qk_norm(blk, gain):
        return blk * lax.rsqrt(_chunk_sums(blk * blk, seg) + EPS) * gain

    lane = lax.broadcasted_iota(jnp.int32, (rows, LANES), 1)
    first_half = (lane % AX_DIM) < (AX_DIM // 2)
    cos = cos_ref[...]
    sin = sin_ref[...]

    def rope(blk):
        partner = jnp.where(first_half, pltpu.roll(blk, LANES - AX_DIM // 2, 1), pltpu.roll(blk, AX_DIM // 2, 1))
        return blk * cos + partner * sin

    for b in range(A_Q_COLS // LANES):
        q = rope(qk_norm(block(b * LANES), gqa_ref[...])) * scale
        qta_ref[b * LANES:(b + 1) * LANES, :] = q.T.astype(_BF16)
    k = rope(qk_norm(block(A_Q_COLS), gka_ref[...])).astype(_BF16)
    vt = block(A_Q_COLS + A_KV_COLS).T.astype(_BF16)
    ones_rows = jnp.ones((BF16_SUBLANES, rows), _BF16)
    one_lane = (lax.broadcasted_iota(jnp.int32, (rows, HEAD_DIM), 1) == 0).astype(_BF16)
    for g in range(A_KV_HEADS):
        ka_ref[g] = jnp.concatenate([k[:, g * HEAD_DIM:(g + 1) * HEAD_DIM], one_lane], axis=1)
        vta_ref[g, 0, 0:HEAD_DIM, :] = vt[g * HEAD_DIM:(g + 1) * HEAD_DIM, :]
        vta_ref[g, 0, HEAD_DIM:HEAD_DIM + BF16_SUBLANES, :] = ones_rows
    pos = pl.program_id(0) * rows + lax.broadcasted_iota(jnp.int32, (rows, LANES), 0)
    pos_feat = jnp.where(lane == 0, (pos // POS_SPLIT * POS_SPLIT).astype(_F32),
                         jnp.where(lane == 1, (pos % POS_SPLIT).astype(_F32),
                                   jnp.where(lane < 5, 1.0, 0.0))).astype(_BF16)
    b_q0 = A_Q_COLS + 2 * A_KV_COLS
    b_k0 = b_q0 + B_QK_COLS
    b_v0 = b_k0 + B_QK_COLS
    for hd in range(B_HEADS):
        q = qk_norm(block(b_q0 + hd * LANES), gqb_ref[...]) * scale
        qtb_ref[hd * LANES:(hd + 1) * LANES, :] = q.T.astype(_BF16)
        kb_ref[:, hd * KB_LANES:hd * KB_LANES + LANES] = qk_norm(block(b_k0 + hd * LANES), gkb_ref[...]).astype(_BF16)
        kb_ref[:, hd * KB_LANES + LANES:(hd + 1) * KB_LANES] = pos_feat
        vtb_ref[hd, 0, 0:B_VDIM, :] = block(b_v0 + hd * LANES).T.astype(_BF16)
        vtb_ref[hd, 0, B_VDIM:B_VDIM + BF16_SUBLANES, :] = ones_rows


def _online_columns(w, k_ref, vt_ref, s_scr, p_scr, bias_fn):
    n = w.shape[1]
    rows = vt_ref.shape[1]
    nk = vt_ref.shape[0]
    assert nk % 2 == 0

    def scores(j, slot):
        k = k_ref[pl.ds(pl.multiple_of(j * KV_CHUNK, KV_CHUNK), KV_CHUNK), :]
        s = _dot(k, w)
        if bias_fn is not None:
            s = s - bias_fn(j)
        s_scr[slot] = s
        return jnp.max(s, axis=0, keepdims=True)

    def weights(slot, m, cmax):
        m_new = jnp.maximum(m, cmax)
        p_scr[slot] = jnp.exp(s_scr[slot] - m_new).astype(_BF16)
        return m_new, jnp.exp(m - m_new)

    def values(j, slot, acc, alpha):
        return alpha * acc + _dot(vt_ref[j], p_scr[slot])

    m = jnp.full((1, n), NEG_BIG, _F32)
    acc = jnp.zeros((rows, n), _F32)
    cmax0 = scores(0, 0)
    cmax1 = scores(1, 1)
    m, alpha = weights(0, m, cmax0)

    def body(i, carry):
        m, alpha, cmax1, acc = carry
        j = 2 * i + 1
        cmax0 = scores(j + 1, 0)
        acc = values(j - 1, 0, acc, alpha)
        m, alpha = weights(1, m, cmax1)
        cmax1 = scores(j + 2, 1)
        acc = values(j, 1, acc, alpha)
        m, alpha = weights(0, m, cmax0)
        return m, alpha, cmax1, acc

    m, alpha, cmax1, acc = lax.fori_loop(0, nk // 2 - 1, body, (m, alpha, cmax1, acc))
    acc = values(nk - 2, 0, acc, alpha)
    m, alpha = weights(1, m, cmax1)
    return values(nk - 1, 1, acc, alpha)


def _sublane_partial_sums(p):
    return jnp.sum(p.reshape(p.shape[0] // 8, 8, p.shape[1]), axis=0)


def _shifted_columns(chunk_fns, n_chunks, k_ref, vt_ref, s_scr, accs, denoms, finish):
    steps = [(t, c) for t in range(len(chunk_fns)) for c in range(n_chunks)]
    half = KV_CHUNK // 2
    halves = [slice(h * half, (h + 1) * half) for h in range(2)]

    def key_rows(j, rows):
        start = j * KV_CHUNK + rows.start
        return pl.ds(start if isinstance(j, int) else pl.multiple_of(start, half), half)

    def scores(t, c, slot, rows):
        j, w = chunk_fns[t](c)
        s_scr[slot, rows, :] = _dot(k_ref[key_rows(j, rows), :], w)

    for rows in halves:
        scores(*steps[0], 0, rows)
    for i, (t, c) in enumerate(steps):
        j, _ = chunk_fns[t](c)
        for rows in halves:
            if i + 1 < len(steps):
                scores(*steps[i + 1], (i + 1) % 2, rows)
            p = jnp.exp(s_scr[i % 2, rows, :])
            if denoms[t] is not None:
                denoms[t] = denoms[t] + _sublane_partial_sums(p)
            accs[t] = accs[t] + _dot(vt_ref[j, 0:accs[t].shape[0], rows], p.astype(_BF16))
        if c == n_chunks - 1:
            finish(t, accs[t], denoms[t])


def _feature_rows(rows_list, n):
    row = lax.broadcasted_iota(jnp.int32, (BF16_SUBLANES, n), 0)
    tile = jnp.zeros((BF16_SUBLANES, n), _F32)
    for r, v in enumerate(rows_list):
        tile = jnp.where(row == r, v, tile)
    return tile.astype(_BF16)


def _tile_cols(t, tq):
    return slice(t * tq, (t + 1) * tq)


def _a_queries(qt_ref, cols):
    return jnp.concatenate([qt_ref[r * HEAD_DIM:(r + 1) * HEAD_DIM, cols] for r in range(A_GROUP)], axis=1)


def _a_finish(acc, o_ref, cols):
    tq = cols.stop - cols.start
    out = acc[0:HEAD_DIM, :] / acc[HEAD_DIM:HEAD_DIM + 1, :]
    for r in range(A_GROUP):
        o_ref[r * HEAD_DIM:(r + 1) * HEAD_DIM, cols] = out[:, r * tq:(r + 1) * tq].astype(o_ref.dtype)


def _attn_a_shift_body(qt_ref, k_ref, vt_ref, bound_ref, o_ref, s_scr):
    n = A_GROUP * A_Q_TOKENS
    feat = _feature_rows([-jnp.broadcast_to(bound_ref[...], (1, n))], n)
    pad = jnp.zeros((KA_LANES - HEAD_DIM - BF16_SUBLANES, n), _BF16)
    ws = [jnp.concatenate([_a_queries(qt_ref, _tile_cols(t, A_Q_TOKENS)), feat, pad], axis=0)
          for t in range(TILES_PER_STEP)]
    _shifted_columns([lambda c, w=w: (c, w) for w in ws], vt_ref.shape[0], k_ref, vt_ref, s_scr,
                     [jnp.zeros((vt_ref.shape[1], n), _F32)] * TILES_PER_STEP, [None] * TILES_PER_STEP,
                     lambda t, acc, _: _a_finish(acc, o_ref, _tile_cols(t, A_Q_TOKENS)))


def _attn_a_online_body(qt_ref, k_ref, vt_ref, o_ref, s_scr, p_scr):
    for t in range(TILES_PER_STEP):
        wq = _a_queries(qt_ref, _tile_cols(t, A_Q_TOKENS))
        w = jnp.concatenate([wq, jnp.zeros((KA_LANES - HEAD_DIM, wq.shape[1]), _BF16)], axis=0)
        _a_finish(_online_columns(w, k_ref, vt_ref, s_scr, p_scr, None), o_ref, _tile_cols(t, A_Q_TOKENS))


def _b_queries(qt_ref, cols):
    zeros = jnp.zeros((HEAD_DIM, cols.stop - cols.start), _BF16)
    return jnp.concatenate([jnp.concatenate([qt_ref[0:HEAD_DIM, cols], zeros], axis=1),
                            jnp.concatenate([zeros, qt_ref[HEAD_DIM:2 * HEAD_DIM, cols]], axis=1)], axis=0)


def _b_tile_start(t):
    return (pl.program_id(1) * TILES_PER_STEP + t) * B_Q_TOKENS


def _b_finish(acc, denom, lq1_ref, lk1_ref, lq2_ref, lk2_ref, subln_ref, o_ref, cols, lambda_init):
    tq = cols.stop - cols.start
    lam = (jnp.exp(jnp.sum(lq1_ref[...] * lk1_ref[...], axis=-1, keepdims=True))
           - jnp.exp(jnp.sum(lq2_ref[...] * lk2_ref[...], axis=-1, keepdims=True)) + lambda_init)
    o0 = acc[0:B_VDIM, 0:tq] / denom[:, 0:tq]
    o1 = acc[0:B_VDIM, tq:2 * tq] / denom[:, tq:2 * tq]
    o = o0 - lam * o1
    o = o * lax.rsqrt(jnp.mean(o * o, axis=0, keepdims=True) + EPS) * subln_ref[...] * (1.0 - lambda_init)
    o_ref[:, cols] = o.astype(o_ref.dtype)


def _attn_b_shift_body(qt_ref, k_ref, vt_ref, bound_ref, slope_ref, lq1_ref, lk1_ref, lq2_ref, lk2_ref,
                       subln_ref, o_ref, w_scr, s_scr, *, lambda_init):
    tq = B_Q_TOKENS
    n = 2 * tq
    bound = jnp.broadcast_to(bound_ref[...], (1, n))
    slope = jnp.broadcast_to(slope_ref[...], (1, n))
    pad = jnp.zeros((KB_LANES - 2 * HEAD_DIM - BF16_SUBLANES, n), _BF16)
    chunk_fns, accs, denoms = [], [], []
    for t in range(TILES_PER_STEP):
        wq = _b_queries(qt_ref, _tile_cols(t, tq))
        q_pos = _b_tile_start(t) + lax.broadcasted_iota(jnp.int32, (1, tq), 1)
        q_hi = (q_pos // POS_SPLIT * POS_SPLIT).astype(_F32)
        q_lo = (q_pos % POS_SPLIT).astype(_F32)
        sq_hi = slope * jnp.concatenate([q_hi, q_hi], axis=1)
        sq_lo = slope * jnp.concatenate([q_lo, q_lo], axis=1)
        before, after = 2 * t, 2 * t + 1
        w_scr[before] = jnp.concatenate([wq, _feature_rows([slope, slope, -sq_hi, -sq_lo, -bound], n), pad], axis=0)
        w_scr[after] = jnp.concatenate([wq, _feature_rows([-slope, -slope, sq_hi, sq_lo, -bound], n), pad], axis=0)
        diag = _b_tile_start(t) // KV_CHUNK
        k_diag = k_ref[pl.ds(pl.multiple_of(diag * KV_CHUNK, KV_CHUNK), KV_CHUNK), :]
        p = jnp.exp(jnp.minimum(_dot(k_diag, w_scr[before]), _dot(k_diag, w_scr[after])))
        accs.append(_dot(vt_ref[diag, 0:B_VDIM, :], p.astype(_BF16)))
        denoms.append(_sublane_partial_sums(p))

        def other_chunk(c, diag=diag, before=before):
            past = jnp.where(c >= diag, 1, 0)
            return c + past, w_scr[before + past]

        chunk_fns.append(other_chunk)

    def finish(t, acc, denom):
        _b_finish(acc, jnp.sum(denom, axis=0, keepdims=True), lq1_ref, lk1_ref, lq2_ref, lk2_ref, subln_ref, o_ref,
                  _tile_cols(t, tq), lambda_init)

    _shifted_columns(chunk_fns, vt_ref.shape[0] - 1, k_ref, vt_ref, s_scr, accs, denoms, finish)


def _attn_b_online_body(qt_ref, k_ref, vt_ref, slope_ref, lq1_ref, lk1_ref, lq2_ref, lk2_ref, subln_ref, o_ref,
                        s_scr, p_scr, *, lambda_init):
    tq = B_Q_TOKENS
    slope = slope_ref[...]
    k_off = lax.broadcasted_iota(jnp.int32, (KV_CHUNK, tq), 0)
    for t in range(TILES_PER_STEP):
        wq = _b_queries(qt_ref, _tile_cols(t, tq))
        w = jnp.concatenate([wq, jnp.zeros((KB_LANES - 2 * HEAD_DIM, 2 * tq), _BF16)], axis=0)
        q_pos = _b_tile_start(t) + lax.broadcasted_iota(jnp.int32, (KV_CHUNK, tq), 1)

        def bias_fn(j, q_pos=q_pos):
            dist = jnp.abs(q_pos - (k_off + j * KV_CHUNK)).astype(_F32)
            bias = slope * dist
            return jnp.concatenate([bias, bias], axis=1)

        acc = _online_columns(w, k_ref, vt_ref, s_scr, p_scr, bias_fn)
        _b_finish(acc, acc[B_VDIM:B_VDIM + 1, :], lq1_ref, lk1_ref, lq2_ref, lk2_ref, subln_ref, o_ref,
                  _tile_cols(t, tq), lambda_init)


def _out_body(x_ref, at_ref, bt_ref, w_out_ref, gain_ref, wg_ref, wu_ref, wd_ref, o_ref):
    tn = (((0,), (0,)), ((), ()))
    half = at_ref.shape[0]
    y = (lax.dot_general(at_ref[...], w_out_ref[0:half, :], tn, preferred_element_type=_F32)
         + lax.dot_general(bt_ref[...], w_out_ref[half:2 * half, :], tn, preferred_element_type=_F32))
    x = x_ref[...] + y
    o_ref[...] = _swiglu_half_step(x, gain_ref[...], wg_ref, wu_ref, wd_ref)


def _resident(shape):
    return pl.BlockSpec(shape, lambda *_: (0,) * len(shape), pipeline_mode=pl.Buffered(1))


def _score_scratch(n):
    return pltpu.VMEM((2, KV_CHUNK, n), _F32)


def _online_scratch(n):
    return [_score_scratch(n), pltpu.VMEM((2, KV_CHUNK, n), _BF16)]


def _params(n_axes):
    return pltpu.CompilerParams(dimension_semantics=("arbitrary",) * n_axes, vmem_limit_bytes=VMEM_LIMIT)


def _ffn_call(x2d, gain, wg, wu, wd):
    s = x2d.shape[0]
    row = pl.BlockSpec((ROW_TILE, D_MODEL), lambda i: (i, 0))
    return pl.pallas_call(
        _ffn_body, name="ffn_half_step",
        grid=(s // ROW_TILE,),
        in_specs=[row, _resident((1, D_MODEL)), _resident(wg.shape), _resident(wu.shape), _resident(wd.shape)],
        out_specs=row,
        out_shape=jax.ShapeDtypeStruct((s, D_MODEL), _F32),
        compiler_params=_params(1),
    )(x2d, gain, wg, wu, wd)


def _prep_call(x2d, gain, w_in, cos, sin, gqa, gka, gqb, gkb, seg):
    s = x2d.shape[0]
    nk = s // KV_CHUNK
    assert ROW_TILE == KV_CHUNK
    row = lambda width: pl.BlockSpec((ROW_TILE, width), lambda i: (i, 0))
    col = lambda height: pl.BlockSpec((height, ROW_TILE), lambda i: (0, i))
    vec = _resident((1, LANES))
    out_shape = (
        jax.ShapeDtypeStruct((A_Q_COLS, s), _BF16),
        jax.ShapeDtypeStruct((A_KV_HEADS, s, KA_LANES), _BF16),
        jax.ShapeDtypeStruct((A_KV_HEADS, nk, HEAD_DIM + BF16_SUBLANES, KV_CHUNK), _BF16),
        jax.ShapeDtypeStruct((B_QK_COLS, s), _BF16),
        jax.ShapeDtypeStruct((s, B_HEADS * KB_LANES), _BF16),
        jax.ShapeDtypeStruct((B_HEADS, nk, B_VDIM + BF16_SUBLANES, KV_CHUNK), _BF16),
    )
    out_specs = (
        col(A_Q_COLS),
        pl.BlockSpec((A_KV_HEADS, ROW_TILE, KA_LANES), lambda i: (0, i, 0)),
        pl.BlockSpec((A_KV_HEADS, 1, HEAD_DIM + BF16_SUBLANES, KV_CHUNK), lambda i: (0, i, 0, 0)),
        col(B_QK_COLS),
        row(B_HEADS * KB_LANES),
        pl.BlockSpec((B_HEADS, 1, B_VDIM + BF16_SUBLANES, KV_CHUNK), lambda i: (0, i, 0, 0)),
    )
    return pl.pallas_call(
        _prep_body, name="in_proj_prep",
        grid=(s // ROW_TILE,),
        in_specs=[row(D_MODEL), _resident((1, D_MODEL)), _resident(w_in.shape), row(LANES), row(LANES),
                  vec, vec, vec, vec, _resident((2 * LANES, LANES))],
        out_specs=out_specs,
        out_shape=out_shape,
        compiler_params=_params(1),
    )(x2d, gain, w_in, cos, sin, gqa, gka, gqb, gkb, seg)


def _attn_a_call(qta, ka, vta, bound, shift_path):
    s = qta.shape[1]
    nk = s // KV_CHUNK
    rows = A_GROUP * HEAD_DIM
    step_tokens = TILES_PER_STEP * A_Q_TOKENS
    in_specs = [pl.BlockSpec((rows, step_tokens), lambda g, i: (g, i)),
                pl.BlockSpec((None, s, KA_LANES), lambda g, i: (g, 0, 0)),
                pl.BlockSpec((None, nk, HEAD_DIM + BF16_SUBLANES, KV_CHUNK), lambda g, i: (g, 0, 0, 0))]
    common = dict(
        grid=(A_KV_HEADS, s // step_tokens),
        out_specs=pl.BlockSpec((rows, step_tokens), lambda g, i: (g, i)),
        out_shape=jax.ShapeDtypeStruct((A_Q_COLS, s), _BF16),
        compiler_params=_params(2))
    if shift_path:
        return pl.pallas_call(_attn_a_shift_body, name="attn_gqa_shift",
                              in_specs=in_specs + [pl.BlockSpec((1, 1), lambda g, i: (0, 0))],
                              scratch_shapes=[_score_scratch(A_GROUP * A_Q_TOKENS)], **common)(qta, ka, vta, bound)
    return pl.pallas_call(_attn_a_online_body, name="attn_gqa_online", in_specs=in_specs,
                          scratch_shapes=_online_scratch(A_GROUP * A_Q_TOKENS), **common)(qta, ka, vta)


def _attn_b_call(qtb, kb, vtb, bound, slopes, lq1, lk1, lq2, lk2, subln, lambda_init, shift_path):
    s = qtb.shape[1]
    nk = s // KV_CHUNK
    n = 2 * B_Q_TOKENS
    per_head = lambda width: pl.BlockSpec((None, 1, width), lambda h, i: (h, 0, 0))
    step_tokens = TILES_PER_STEP * B_Q_TOKENS
    qkv_specs = [pl.BlockSpec((2 * HEAD_DIM, step_tokens), lambda h, i: (h, i)),
                 pl.BlockSpec((s, KB_LANES), lambda h, i: (0, h)),
                 pl.BlockSpec((None, nk, B_VDIM + BF16_SUBLANES, KV_CHUNK), lambda h, i: (h, 0, 0, 0))]
    param_specs = [per_head(1), per_head(HEAD_DIM), per_head(HEAD_DIM), per_head(HEAD_DIM), per_head(HEAD_DIM),
                   pl.BlockSpec((B_VDIM, 1), lambda h, i: (0, 0))]
    common = dict(
        grid=(B_HEADS, s // step_tokens),
        out_specs=pl.BlockSpec((B_VDIM, step_tokens), lambda h, i: (h, i)),
        out_shape=jax.ShapeDtypeStruct((B_V_COLS, s), _BF16),
        compiler_params=_params(2))
    params = (slopes, lq1, lk1, lq2, lk2, subln)
    if shift_path:
        return pl.pallas_call(
            functools.partial(_attn_b_shift_body, lambda_init=lambda_init), name="attn_diff_shift",
            in_specs=qkv_specs + [pl.BlockSpec((1, 1), lambda h, i: (0, 0))] + param_specs,
            scratch_shapes=[pltpu.VMEM((2 * TILES_PER_STEP, KB_LANES, n), _BF16), _score_scratch(n)],
            **common)(qtb, kb, vtb, bound, *params)
    return pl.pallas_call(
        functools.partial(_attn_b_online_body, lambda_init=lambda_init), name="attn_diff_online",
        in_specs=qkv_specs + param_specs, scratch_shapes=_online_scratch(n), **common)(qtb, kb, vtb, *params)


def _out_call(x2d, at, bt, w_out, gain, wg, wu, wd):
    s = x2d.shape[0]
    row = pl.BlockSpec((ROW_TILE, D_MODEL), lambda i: (i, 0))
    col = pl.BlockSpec((at.shape[0], ROW_TILE), lambda i: (0, i))
    return pl.pallas_call(
        _out_body, name="out_proj_ffn",
        grid=(s // ROW_TILE,),
        in_specs=[row, col, col, _resident(w_out.shape), _resident((1, D_MODEL)),
                  _resident(wg.shape), _resident(wu.shape), _resident(wd.shape)],
        out_specs=row,
        out_shape=jax.ShapeDtypeStruct((s, D_MODEL), _F32),
        compiler_params=_params(1),
    )(x2d, at, bt, w_out, gain, wg, wu, wd)


def _rope_tables(s):
    rows = s // GRID_W
    row = jnp.broadcast_to(jnp.arange(rows, dtype=_F32)[:, None], (rows, GRID_W)).reshape(-1)
    colp = jnp.broadcast_to(jnp.arange(GRID_W, dtype=_F32)[None, :], (rows, GRID_W)).reshape(-1)
    inv_freq = ROPE_THETA ** (-jnp.arange(0, AX_DIM, 2, dtype=_F32) / AX_DIM)
    ang_r = row[:, None] * inv_freq[None, :]
    ang_c = colp[:, None] * inv_freq[None, :]
    cos = jnp.concatenate([jnp.cos(ang_r)] * 2 + [jnp.cos(ang_c)] * 2, axis=-1)
    sin = jnp.concatenate([-jnp.sin(ang_r), jnp.sin(ang_r), -jnp.sin(ang_c), jnp.sin(ang_c)], axis=-1)
    return jnp.tile(cos, (1, LANES // HEAD_DIM)), jnp.tile(sin, (1, LANES // HEAD_DIM))


def _score_bound(q_gain, k_gain):
    bound = math.sqrt(HEAD_DIM) * jnp.max(jnp.abs(q_gain)) * jnp.max(jnp.abs(k_gain)) * BOUND_MARGIN
    return bound.astype(_F32).reshape(1, 1)


def kernel(x, ffn1_norm, ffn1_w_gate, ffn1_w_up, ffn1_w_down, attn_norm, w_in, a_q_norm, a_k_norm, b_q_norm, b_k_norm, lambda_q1, lambda_k1, lambda_q2, lambda_k2, b_subln, w_out, ffn2_norm, ffn2_w_gate, ffn2_w_up, ffn2_w_down):
    bsz, s, _ = x.shape
    depth = w_in.shape[0]
    assert s % ROW_TILE == 0 and s % KV_CHUNK == 0 and s % (TILES_PER_STEP * B_Q_TOKENS) == 0
    assert KV_CHUNK % B_Q_TOKENS == 0 and s < POS_SPLIT * 256
    cos, sin = _rope_tables(s)
    lane_chunk = jnp.arange(LANES) // HEAD_DIM
    seg = ((lane_chunk[:, None] == lane_chunk[None, :]).astype(_F32) / HEAD_DIM).astype(_BF16)
    seg = jnp.concatenate([seg, seg], axis=0)
    slopes = (2.0 ** (-8.0 * (jnp.arange(B_HEADS, dtype=_F32) + 1.0) / B_HEADS)).reshape(B_HEADS, 1, 1)
    lane_gain = lambda g: jnp.tile(g, LANES // HEAD_DIM).reshape(1, LANES)
    bf = lambda w: w.astype(_BF16)

    outs = []
    for b in range(bsz):
        xb = x[b]
        for l in range(depth):
            lambda_init = 0.8 - 0.6 * math.exp(-0.3 * l)
            xb = _ffn_call(xb, ffn1_norm[l].reshape(1, D_MODEL), bf(ffn1_w_gate[l]), bf(ffn1_w_up[l]), bf(ffn1_w_down[l]))
            qta, ka, vta, qtb, kb, vtb = _prep_call(
                xb, attn_norm[l].reshape(1, D_MODEL), bf(w_in[l]), cos, sin,
                lane_gain(a_q_norm[l]), lane_gain(a_k_norm[l]), lane_gain(b_q_norm[l]), lane_gain(b_k_norm[l]), seg)
            per_head = lambda p: p[l].reshape(B_HEADS, 1, HEAD_DIM)
            b_params = (slopes, per_head(lambda_q1), per_head(lambda_k1), per_head(lambda_q2), per_head(lambda_k2),
                        b_subln[l].reshape(B_VDIM, 1), lambda_init)

            bound_a = _score_bound(a_q_norm[l], a_k_norm[l])
            bound_b = _score_bound(b_q_norm[l], b_k_norm[l])

            def mixers(shift_path, qta=qta, ka=ka, vta=vta, qtb=qtb, kb=kb, vtb=vtb, b_params=b_params,
                       bound_a=bound_a, bound_b=bound_b):
                return (_attn_a_call(qta, ka, vta, bound_a, shift_path),
                        _attn_b_call(qtb, kb, vtb, bound_b, *b_params, shift_path))

            at, bt = lax.cond(jnp.maximum(bound_a, bound_b)[0, 0] <= SHIFT_PATH_MAX_BOUND,
                              lambda: mixers(True), lambda: mixers(False))
            xb = _out_call(xb, at, bt, bf(w_out[l]), ffn2_norm[l].reshape(1, D_MODEL),
                           bf(ffn2_w_gate[l]), bf(ffn2_w_up[l]), bf(ffn2_w_down[l]))
        outs.append(xb)
    return jnp.stack(outs, axis=0)
```

```python
import functools
import math

import jax
import jax.numpy as jnp
from jax import lax
from jax.experimental import pallas as pl
from jax.experimental.pallas import tpu as pltpu

D_MODEL = 1024
HEAD_DIM = 64
GRID_W = 64
EPS = 1e-6
ROPE_THETA = 10000.0
AX_DIM = HEAD_DIM // 2
A_HEADS = 8
A_KV_HEADS = 2
A_GROUP = A_HEADS // A_KV_HEADS
B_HEADS = 4
B_VDIM = 2 * HEAD_DIM
A_Q_COLS = A_HEADS * HEAD_DIM
A_KV_COLS = A_KV_HEADS * HEAD_DIM
B_QK_COLS = B_HEADS * 2 * HEAD_DIM
B_V_COLS = B_HEADS * B_VDIM
IN_COLS = A_Q_COLS + 2 * A_KV_COLS + 2 * B_QK_COLS + B_V_COLS
D_FF = 2816

LANES = 128
BF16_SUBLANES = 16
ROW_TILE = 512
KV_CHUNK = 512
A_Q_TOKENS = 128
B_Q_TOKENS = 256
TILES_PER_STEP = 2
FF_SPLITS = (0, 1536, D_FF)
VMEM_LIMIT = 56 * 1024 * 1024
NEG_BIG = -1e30

KA_LANES = 2 * HEAD_DIM
KB_LANES = 4 * HEAD_DIM
POS_SPLIT = 128
BOUND_MARGIN = 1.0 + 2.0 ** -6
SHIFT_PATH_MAX_BOUND = 40.0

_F32 = jnp.float32
_BF16 = jnp.bfloat16


def _dot(a, b):
    return jnp.dot(a, b, preferred_element_type=_F32)


def _rms_rows(x, gain):
    return x * lax.rsqrt(jnp.mean(x * x, axis=-1, keepdims=True) + EPS) * gain


def _swiglu_half_step(x, gain, wg_ref, wu_ref, wd_ref):
    h = _rms_rows(x, gain).astype(_BF16)
    acc = jnp.zeros(x.shape, _F32)
    for lo, hi in zip(FF_SPLITS[:-1], FF_SPLITS[1:]):
        cols = slice(lo, hi)
        g = _dot(h, wg_ref[:, cols])
        u = _dot(h, wu_ref[:, cols])
        a = (g / (1.0 + jnp.exp(-g)) * u).astype(_BF16)
        acc = acc + _dot(a, wd_ref[cols, :])
    return x + 0.5 * acc


def _ffn_body(x_ref, gain_ref, wg_ref, wu_ref, wd_ref, o_ref):
    o_ref[...] = _swiglu_half_step(x_ref[...], gain_ref[...], wg_ref, wu_ref, wd_ref)


def _chunk_sums(x, seg):
    hi = x.astype(_BF16)
    lo = (x - hi.astype(_F32)).astype(_BF16)
    return _dot(jnp.concatenate([hi, lo], axis=1), seg)


def _prep_body(x_ref, gain_ref, w_in_ref, cos_ref, sin_ref, gqa_ref, gka_ref, gqb_ref, gkb_ref, seg_ref,
               qta_ref, ka_ref, vta_ref, qtb_ref, kb_ref, vtb_ref):
    rows = x_ref.shape[0]
    h = _rms_rows(x_ref[...], gain_ref[...]).astype(_BF16)
    proj = _dot(h, w_in_ref[...])
    seg = seg_ref[...]
    scale = 1.0 / math.sqrt(HEAD_DIM)

    def block(col):
        return proj[:, col:col + LANES]

    def qk_norm(blk, gain):
        return blk * lax.rsqrt(_chunk_sums(blk * blk, seg) + EPS) * gain

    lane = lax.broadcasted_iota(jnp.int32, (rows, LANES), 1)
    first_half = (lane % AX_DIM) < (AX_DIM // 2)
    cos = cos_ref[...]
    sin = sin_ref[...]

    def rope(blk):
        partner = jnp.where(first_half, pltpu.roll(blk, LANES - AX_DIM // 2, 1), pltpu.roll(blk, AX_DIM // 2, 1))
        return blk * cos + partner * sin

    for b in range(A_Q_COLS // LANES):
        q = rope(qk_norm(block(b * LANES), gqa_ref[...])) * scale
        qta_ref[b * LANES:(b + 1) * LANES, :] = q.T.astype(_BF16)
    k = rope(qk_norm(block(A_Q_COLS), gka_ref[...])).astype(_BF16)
    vt = block(A_Q_COLS + A_KV_COLS).T.astype(_BF16)
    ones_rows = jnp.ones((BF16_SUBLANES, rows), _BF16)
    one_lane = (lax.broadcasted_iota(jnp.int32, (rows, HEAD_DIM), 1) == 0).astype(_BF16)
    for g in range(A_KV_HEADS):
        ka_ref[g] = jnp.concatenate([k[:, g * HEAD_DIM:(g + 1) * HEAD_DIM], one_lane], axis=1)
        vta_ref[g, 0, 0:HEAD_DIM, :] = vt[g * HEAD_DIM:(g + 1) * HEAD_DIM, :]
        vta_ref[g, 0, HEAD_DIM:HEAD_DIM + BF16_SUBLANES, :] = ones_rows
    pos = pl.program_id(0) * rows + lax.broadcasted_iota(jnp.int32, (rows, LANES), 0)
    pos_feat = jnp.where(lane == 0, (pos // POS_SPLIT * POS_SPLIT).astype(_F32),
                         jnp.where(lane == 1, (pos % POS_SPLIT).astype(_F32),
                                   jnp.where(lane < 5, 1.0, 0.0))).astype(_BF16)
    b_q0 = A_Q_COLS + 2 * A_KV_COLS
    b_k0 = b_q0 + B_QK_COLS
    b_v0 = b_k0 + B_QK_COLS
    for hd in range(B_HEADS):
        q = qk_norm(block(b_q0 + hd * LANES), gqb_ref[...]) * scale
        qtb_ref[hd * LANES:(hd + 1) * LANES, :] = q.T.astype(_BF16)
        kb_ref[:, hd * KB_LANES:hd * KB_LANES + LANES] = qk_norm(block(b_k0 + hd * LANES), gkb_ref[...]).astype(_BF16)
        kb_ref[:, hd * KB_LANES + LANES:(hd + 1) * KB_LANES] = pos_feat
        vtb_ref[hd, 0, 0:B_VDIM, :] = block(b_v0 + hd * LANES).T.astype(_BF16)
        vtb_ref[hd, 0, B_VDIM:B_VDIM + BF16_SUBLANES, :] = ones_rows


def _online_columns(w, k_ref, vt_ref, s_scr, p_scr, bias_fn):
    n = w.shape[1]
    rows = vt_ref.shape[1]
    nk = vt_ref.shape[0]
    assert nk % 2 == 0

    def scores(j, slot):
        k = k_ref[pl.ds(pl.multiple_of(j * KV_CHUNK, KV_CHUNK), KV_CHUNK), :]
        s = _dot(k, w)
        if bias_fn is not None:
            s = s - bias_fn(j)
        s_scr[slot] = s
        return jnp.max(s, axis=0, keepdims=True)

    def weights(slot, m, cmax):
        m_new = jnp.maximum(m, cmax)
        p_scr[slot] = jnp.exp(s_scr[slot] - m_new).astype(_BF16)
        return m_new, jnp.exp(m - m_new)

    def values(j, slot, acc, alpha):
        return alpha * acc + _dot(vt_ref[j], p_scr[slot])

    m = jnp.full((1, n), NEG_BIG, _F32)
    acc = jnp.zeros((rows, n), _F32)
    cmax0 = scores(0, 0)
    cmax1 = scores(1, 1)
    m, alpha = weights(0, m, cmax0)

    def body(i, carry):
        m, alpha, cmax1, acc = carry
        j = 2 * i + 1
        cmax0 = scores(j + 1, 0)
        acc = values(j - 1, 0, acc, alpha)
        m, alpha = weights(1, m, cmax1)
        cmax1 = scores(j + 2, 1)
        acc = values(j, 1, acc, alpha)
        m, alpha = weights(0, m, cmax0)
        return m, alpha, cmax1, acc

    m, alpha, cmax1, acc = lax.fori_loop(0, nk // 2 - 1, body, (m, alpha, cmax1, acc))
    acc = values(nk - 2, 0, acc, alpha)
    m, alpha = weights(1, m, cmax1)
    return values(nk - 1, 1, acc, alpha)


def _sublane_partial_sums(p):
    return jnp.sum(p.reshape(p.shape[0] // 8, 8, p.shape[1]), axis=0)


def _shifted_columns(chunk_fns, n_chunks, k_ref, vt_ref, s_scr, accs, denoms, finish):
    steps = [(t, c) for t in range(len(chunk_fns)) for c in range(n_chunks)]
    half = KV_CHUNK // 2
    halves = [slice(h * half, (h + 1) * half) for h in range(2)]

    def key_rows(j, rows):
        start = j * KV_CHUNK + rows.start
        return pl.ds(start if isinstance(j, int) else pl.multiple_of(start, half), half)

    def scores(t, c, slot, rows):
        j, w = chunk_fns[t](c)
        k = k_ref[key_rows(j, rows), :]
        s_scr[slot, rows, :] = jnp.minimum(_dot(k, w[0]), _dot(k, w[1])) if isinstance(w, tuple) else _dot(k, w)

    for rows in halves:
        scores(*steps[0], 0, rows)
    for i, (t, c) in enumerate(steps):
        j, _ = chunk_fns[t](c)
        for rows in halves:
            if i + 1 < len(steps):
                scores(*steps[i + 1], (i + 1) % 2, rows)
            p = jnp.exp(s_scr[i % 2, rows, :])
            if denoms[t] is not None:
                denoms[t] = denoms[t] + _sublane_partial_sums(p)
            accs[t] = accs[t] + _dot(vt_ref[j, 0:accs[t].shape[0], rows], p.astype(_BF16))
        if c == n_chunks - 1:
            finish(t, accs[t], denoms[t])


def _feature_rows(rows_list, n):
    row = lax.broadcasted_iota(jnp.int32, (BF16_SUBLANES, n), 0)
    tile = jnp.zeros((BF16_SUBLANES, n), _F32)
    for r, v in enumerate(rows_list):
        tile = jnp.where(row == r, v, tile)
    return tile.astype(_BF16)


def _tile_cols(t, tq):
    return slice(t * tq, (t + 1) * tq)


def _a_queries(qt_ref, cols):
    return jnp.concatenate([qt_ref[r * HEAD_DIM:(r + 1) * HEAD_DIM, cols] for r in range(A_GROUP)], axis=1)


def _a_finish(acc, o_ref, cols):
    tq = cols.stop - cols.start
    out = acc[0:HEAD_DIM, :] / acc[HEAD_DIM:HEAD_DIM + 1, :]
    for r in range(A_GROUP):
        o_ref[r * HEAD_DIM:(r + 1) * HEAD_DIM, cols] = out[:, r * tq:(r + 1) * tq].astype(o_ref.dtype)


def _attn_a_shift_body(qt_ref, k_ref, vt_ref, bound_ref, o_ref, s_scr):
    n = A_GROUP * A_Q_TOKENS
    feat = _feature_rows([-jnp.broadcast_to(bound_ref[...], (1, n))], n)
    pad = jnp.zeros((KA_LANES - HEAD_DIM - BF16_SUBLANES, n), _BF16)
    ws = [jnp.concatenate([_a_queries(qt_ref, _tile_cols(t, A_Q_TOKENS)), feat, pad], axis=0)
          for t in range(TILES_PER_STEP)]
    _shifted_columns([lambda c, w=w: (c, w) for w in ws], vt_ref.shape[0], k_ref, vt_ref, s_scr,
                     [jnp.zeros((vt_ref.shape[1], n), _F32)] * TILES_PER_STEP, [None] * TILES_PER_STEP,
                     lambda t, acc, _: _a_finish(acc, o_ref, _tile_cols(t, A_Q_TOKENS)))


def _attn_a_online_body(qt_ref, k_ref, vt_ref, o_ref, s_scr, p_scr):
    for t in range(TILES_PER_STEP):
        wq = _a_queries(qt_ref, _tile_cols(t, A_Q_TOKENS))
        w = jnp.concatenate([wq, jnp.zeros((KA_LANES - HEAD_DIM, wq.shape[1]), _BF16)], axis=0)
        _a_finish(_online_columns(w, k_ref, vt_ref, s_scr, p_scr, None), o_ref, _tile_cols(t, A_Q_TOKENS))


def _b_queries(qt_ref, cols):
    zeros = jnp.zeros((HEAD_DIM, cols.stop - cols.start), _BF16)
    return jnp.concatenate([jnp.concatenate([qt_ref[0:HEAD_DIM, cols], zeros], axis=1),
                            jnp.concatenate([zeros, qt_ref[HEAD_DIM:2 * HEAD_DIM, cols]], axis=1)], axis=0)


def _b_tile_start(t):
    return (pl.program_id(1) * TILES_PER_STEP + t) * B_Q_TOKENS


def _b_finish(acc, denom, lq1_ref, lk1_ref, lq2_ref, lk2_ref, subln_ref, o_ref, cols, lambda_init):
    tq = cols.stop - cols.start
    lam = (jnp.exp(jnp.sum(lq1_ref[...] * lk1_ref[...], axis=-1, keepdims=True))
           - jnp.exp(jnp.sum(lq2_ref[...] * lk2_ref[...], axis=-1, keepdims=True)) + lambda_init)
    o0 = acc[0:B_VDIM, 0:tq] / denom[:, 0:tq]
    o1 = acc[0:B_VDIM, tq:2 * tq] / denom[:, tq:2 * tq]
    o = o0 - lam * o1
    o = o * lax.rsqrt(jnp.mean(o * o, axis=0, keepdims=True) + EPS) * subln_ref[...] * (1.0 - lambda_init)
    o_ref[:, cols] = o.astype(o_ref.dtype)


def _attn_b_shift_body(qt_ref, k_ref, vt_ref, bound_ref, slope_ref, lq1_ref, lk1_ref, lq2_ref, lk2_ref,
                       subln_ref, o_ref, w_scr, s_scr, *, lambda_init):
    tq = B_Q_TOKENS
    n = 2 * tq
    bound = jnp.broadcast_to(bound_ref[...], (1, n))
    slope = jnp.broadcast_to(slope_ref[...], (1, n))
    pad = jnp.zeros((KB_LANES - 2 * HEAD_DIM - BF16_SUBLANES, n), _BF16)
    chunk_fns, accs, denoms = [], [], []
    for t in range(TILES_PER_STEP):
        wq = _b_queries(qt_ref, _tile_cols(t, tq))
        q_pos = _b_tile_start(t) + lax.broadcasted_iota(jnp.int32, (1, tq), 1)
        q_hi = (q_pos // POS_SPLIT * POS_SPLIT).astype(_F32)
        q_lo = (q_pos % POS_SPLIT).astype(_F32)
        sq_hi = slope * jnp.concatenate([q_hi, q_hi], axis=1)
        sq_lo = slope * jnp.concatenate([q_lo, q_lo], axis=1)
        before, after = 2 * t, 2 * t + 1
        w_scr[before] = jnp.concatenate([wq, _feature_rows([slope, slope, -sq_hi, -sq_lo, -bound], n), pad], axis=0)
        w_scr[after] = jnp.concatenate([wq, _feature_rows([-slope, -slope, sq_hi, sq_lo, -bound], n), pad], axis=0)
        accs.append(jnp.zeros((B_VDIM, n), _F32))
        denoms.append(jnp.zeros((8, n), _F32))
        diag = _b_tile_start(t) // KV_CHUNK

        def tile_chunk(c, diag=diag, before=before, after=after):
            if c == 0:
                return diag, (w_scr[before], w_scr[after])
            past = jnp.where(c - 1 >= diag, 1, 0)
            return c - 1 + past, w_scr[before + past]

        chunk_fns.append(tile_chunk)

    def finish(t, acc, denom):
        _b_finish(acc, jnp.sum(denom, axis=0, keepdims=True), lq1_ref, lk1_ref, lq2_ref, lk2_ref, subln_ref, o_ref,
                  _tile_cols(t, tq), lambda_init)

    _shifted_columns(chunk_fns, vt_ref.shape[0], k_ref, vt_ref, s_scr, accs, denoms, finish)


def _attn_b_online_body(qt_ref, k_ref, vt_ref, slope_ref, lq1_ref, lk1_ref, lq2_ref, lk2_ref, subln_ref, o_ref,
                        s_scr, p_scr, *, lambda_init):
    tq = B_Q_TOKENS
    slope = slope_ref[...]
    k_off = lax.broadcasted_iota(jnp.int32, (KV_CHUNK, tq), 0)
    for t in range(TILES_PER_STEP):
        wq = _b_queries(qt_ref, _tile_cols(t, tq))
        w = jnp.concatenate([wq, jnp.zeros((KB_LANES - 2 * HEAD_DIM, 2 * tq), _BF16)], axis=0)
        q_pos = _b_tile_start(t) + lax.broadcasted_iota(jnp.int32, (KV_CHUNK, tq), 1)

        def bias_fn(j, q_pos=q_pos):
            dist = jnp.abs(q_pos - (k_off + j * KV_CHUNK)).astype(_F32)
            bias = slope * dist
            return jnp.concatenate([bias, bias], axis=1)

        acc = _online_columns(w, k_ref, vt_ref, s_scr, p_scr, bias_fn)
        _b_finish(acc, acc[B_VDIM:B_VDIM + 1, :], lq1_ref, lk1_ref, lq2_ref, lk2_ref, subln_ref, o_ref,
                  _tile_cols(t, tq), lambda_init)


def _out_body(x_ref, at_ref, bt_ref, w_out_ref, gain_ref, wg_ref, wu_ref, wd_ref, o_ref):
    tn = (((0,), (0,)), ((), ()))
    half = at_ref.shape[0]
    y = (lax.dot_general(at_ref[...], w_out_ref[0:half, :], tn, preferred_element_type=_F32)
         + lax.dot_general(bt_ref[...], w_out_ref[half:2 * half, :], tn, preferred_element_type=_F32))
    x = x_ref[...] + y
    o_ref[...] = _swiglu_half_step(x, gain_ref[...], wg_ref, wu_ref, wd_ref)


def _resident(shape):
    return pl.BlockSpec(shape, lambda *_: (0,) * len(shape), pipeline_mode=pl.Buffered(1))


def _score_scratch(n):
    return pltpu.VMEM((2, KV_CHUNK, n), _F32)


def _online_scratch(n):
    return [_score_scratch(n), pltpu.VMEM((2, KV_CHUNK, n), _BF16)]


def _params(n_axes):
    return pltpu.CompilerParams(dimension_semantics=("arbitrary",) * n_axes, vmem_limit_bytes=VMEM_LIMIT)


def _ffn_call(x2d, gain, wg, wu, wd):
    s = x2d.shape[0]
    row = pl.BlockSpec((ROW_TILE, D_MODEL), lambda i: (i, 0))
    return pl.pallas_call(
        _ffn_body, name="ffn_half_step",
        grid=(s // ROW_TILE,),
        in_specs=[row, _resident((1, D_MODEL)), _resident(wg.shape), _resident(wu.shape), _resident(wd.shape)],
        out_specs=row,
        out_shape=jax.ShapeDtypeStruct((s, D_MODEL), _F32),
        compiler_params=_params(1),
    )(x2d, gain, wg, wu, wd)


def _prep_call(x2d, gain, w_in, cos, sin, gqa, gka, gqb, gkb, seg):
    s = x2d.shape[0]
    nk = s // KV_CHUNK
    assert ROW_TILE == KV_CHUNK
    row = lambda width: pl.BlockSpec((ROW_TILE, width), lambda i: (i, 0))
    col = lambda height: pl.BlockSpec((height, ROW_TILE), lambda i: (0, i))
    vec = _resident((1, LANES))
    out_shape = (
        jax.ShapeDtypeStruct((A_Q_COLS, s), _BF16),
        jax.ShapeDtypeStruct((A_KV_HEADS, s, KA_LANES), _BF16),
        jax.ShapeDtypeStruct((A_KV_HEADS, nk, HEAD_DIM + BF16_SUBLANES, KV_CHUNK), _BF16),
        jax.ShapeDtypeStruct((B_QK_COLS, s), _BF16),
        jax.ShapeDtypeStruct((s, B_HEADS * KB_LANES), _BF16),
        jax.ShapeDtypeStruct((B_HEADS, nk, B_VDIM + BF16_SUBLANES, KV_CHUNK), _BF16),
    )
    out_specs = (
        col(A_Q_COLS),
        pl.BlockSpec((A_KV_HEADS, ROW_TILE, KA_LANES), lambda i: (0, i, 0)),
        pl.BlockSpec((A_KV_HEADS, 1, HEAD_DIM + BF16_SUBLANES, KV_CHUNK), lambda i: (0, i, 0, 0)),
        col(B_QK_COLS),
        row(B_HEADS * KB_LANES),
        pl.BlockSpec((B_HEADS, 1, B_VDIM + BF16_SUBLANES, KV_CHUNK), lambda i: (0, i, 0, 0)),
    )
    return pl.pallas_call(
        _prep_body, name="in_proj_prep",
        grid=(s // ROW_TILE,),
        in_specs=[row(D_MODEL), _resident((1, D_MODEL)), _resident(w_in.shape), row(LANES), row(LANES),
                  vec, vec, vec, vec, _resident((2 * LANES, LANES))],
        out_specs=out_specs,
        out_shape=out_shape,
        compiler_params=_params(1),
    )(x2d, gain, w_in, cos, sin, gqa, gka, gqb, gkb, seg)


def _attn_a_call(qta, ka, vta, bound, shift_path):
    s = qta.shape[1]
    nk = s // KV_CHUNK
    rows = A_GROUP * HEAD_DIM
    step_tokens = TILES_PER_STEP * A_Q_TOKENS
    in_specs = [pl.BlockSpec((rows, step_tokens), lambda g, i: (g, i)),
                pl.BlockSpec((None, s, KA_LANES), lambda g, i: (g, 0, 0)),
                pl.BlockSpec((None, nk, HEAD_DIM + BF16_SUBLANES, KV_CHUNK), lambda g, i: (g, 0, 0, 0))]
    common = dict(
        grid=(A_KV_HEADS, s // step_tokens),
        out_specs=pl.BlockSpec((rows, step_tokens), lambda g, i: (g, i)),
        out_shape=jax.ShapeDtypeStruct((A_Q_COLS, s), _BF16),
        compiler_params=_params(2))
    if shift_path:
        return pl.pallas_call(_attn_a_shift_body, name="attn_gqa_shift",
                              in_specs=in_specs + [pl.BlockSpec((1, 1), lambda g, i: (0, 0))],
                              scratch_shapes=[_score_scratch(A_GROUP * A_Q_TOKENS)], **common)(qta, ka, vta, bound)
    return pl.pallas_call(_attn_a_online_body, name="attn_gqa_online", in_specs=in_specs,
                          scratch_shapes=_online_scratch(A_GROUP * A_Q_TOKENS), **common)(qta, ka, vta)


def _attn_b_call(qtb, kb, vtb, bound, slopes, lq1, lk1, lq2, lk2, subln, lambda_init, shift_path):
    s = qtb.shape[1]
    nk = s // KV_CHUNK
    n = 2 * B_Q_TOKENS
    per_head = lambda width: pl.BlockSpec((None, 1, width), lambda h, i: (h, 0, 0))
    step_tokens = TILES_PER_STEP * B_Q_TOKENS
    qkv_specs = [pl.BlockSpec((2 * HEAD_DIM, step_tokens), lambda h, i: (h, i)),
                 pl.BlockSpec((s, KB_LANES), lambda h, i: (0, h)),
                 pl.BlockSpec((None, nk, B_VDIM + BF16_SUBLANES, KV_CHUNK), lambda h, i: (h, 0, 0, 0))]
    param_specs = [per_head(1), per_head(HEAD_DIM), per_head(HEAD_DIM), per_head(HEAD_DIM), per_head(HEAD_DIM),
                   pl.BlockSpec((B_VDIM, 1), lambda h, i: (0, 0))]
    common = dict(
        grid=(B_HEADS, s // step_tokens),
        out_specs=pl.BlockSpec((B_VDIM, step_tokens), lambda h, i: (h, i)),
        out_shape=jax.ShapeDtypeStruct((B_V_COLS, s), _BF16),
        compiler_params=_params(2))
    params = (slopes, lq1, lk1, lq2, lk2, subln)
    if shift_path:
        return pl.pallas_call(
            functools.partial(_attn_b_shift_body, lambda_init=lambda_init), name="attn_diff_shift",
            in_specs=qkv_specs + [pl.BlockSpec((1, 1), lambda h, i: (0, 0))] + param_specs,
            scratch_shapes=[pltpu.VMEM((2 * TILES_PER_STEP, KB_LANES, n), _BF16), _score_scratch(n)],
            **common)(qtb, kb, vtb, bound, *params)
    return pl.pallas_call(
        functools.partial(_attn_b_online_body, lambda_init=lambda_init), name="attn_diff_online",
        in_specs=qkv_specs + param_specs, scratch_shapes=_online_scratch(n), **common)(qtb, kb, vtb, *params)


def _out_call(x2d, at, bt, w_out, gain, wg, wu, wd):
    s = x2d.shape[0]
    row = pl.BlockSpec((ROW_TILE, D_MODEL), lambda i: (i, 0))
    col = pl.BlockSpec((at.shape[0], ROW_TILE), lambda i: (0, i))
    return pl.pallas_call(
        _out_body, name="out_proj_ffn",
        grid=(s // ROW_TILE,),
        in_specs=[row, col, col, _resident(w_out.shape), _resident((1, D_MODEL)),
                  _resident(wg.shape), _resident(wu.shape), _resident(wd.shape)],
        out_specs=row,
        out_shape=jax.ShapeDtypeStruct((s, D_MODEL), _F32),
        compiler_params=_params(1),
    )(x2d, at, bt, w_out, gain, wg, wu, wd)


def _rope_tables(s):
    rows = s // GRID_W
    row = jnp.broadcast_to(jnp.arange(rows, dtype=_F32)[:, None], (rows, GRID_W)).reshape(-1)
    colp = jnp.broadcast_to(jnp.arange(GRID_W, dtype=_F32)[None, :], (rows, GRID_W)).reshape(-1)
    inv_freq = ROPE_THETA ** (-jnp.arange(0, AX_DIM, 2, dtype=_F32) / AX_DIM)
    ang_r = row[:, None] * inv_freq[None, :]
    ang_c = colp[:, None] * inv_freq[None, :]
    cos = jnp.concatenate([jnp.cos(ang_r)] * 2 + [jnp.cos(ang_c)] * 2, axis=-1)
    sin = jnp.concatenate([-jnp.sin(ang_r), jnp.sin(ang_r), -jnp.sin(ang_c), jnp.sin(ang_c)], axis=-1)
    return jnp.tile(cos, (1, LANES // HEAD_DIM)), jnp.tile(sin, (1, LANES // HEAD_DIM))


def _score_bound(q_gain, k_gain):
    bound = math.sqrt(HEAD_DIM) * jnp.max(jnp.abs(q_gain)) * jnp.max(jnp.abs(k_gain)) * BOUND_MARGIN
    return bound.astype(_F32).reshape(1, 1)


def kernel(x, ffn1_norm, ffn1_w_gate, ffn1_w_up, ffn1_w_down, attn_norm, w_in, a_q_norm, a_k_norm, b_q_norm, b_k_norm, lambda_q1, lambda_k1, lambda_q2, lambda_k2, b_subln, w_out, ffn2_norm, ffn2_w_gate, ffn2_w_up, ffn2_w_down):
    bsz, s, _ = x.shape
    depth = w_in.shape[0]
    assert s % ROW_TILE == 0 and s % KV_CHUNK == 0 and s % (TILES_PER_STEP * B_Q_TOKENS) == 0
    assert KV_CHUNK % B_Q_TOKENS == 0 and s < POS_SPLIT * 256
    cos, sin = _rope_tables(s)
    lane_chunk = jnp.arange(LANES) // HEAD_DIM
    seg = ((lane_chunk[:, None] == lane_chunk[None, :]).astype(_F32) / HEAD_DIM).astype(_BF16)
    seg = jnp.concatenate([seg, seg], axis=0)
    slopes = (2.0 ** (-8.0 * (jnp.arange(B_HEADS, dtype=_F32) + 1.0) / B_HEADS)).reshape(B_HEADS, 1, 1)
    lane_gain = lambda g: jnp.tile(g, LANES // HEAD_DIM).reshape(1, LANES)
    bf = lambda w: w.astype(_BF16)

    outs = []
    for b in range(bsz):
        xb = x[b]
        for l in range(depth):
            lambda_init = 0.8 - 0.6 * math.exp(-0.3 * l)
            xb = _ffn_call(xb, ffn1_norm[l].reshape(1, D_MODEL), bf(ffn1_w_gate[l]), bf(ffn1_w_up[l]), bf(ffn1_w_down[l]))
            qta, ka, vta, qtb, kb, vtb = _prep_call(
                xb, attn_norm[l].reshape(1, D_MODEL), bf(w_in[l]), cos, sin,
                lane_gain(a_q_norm[l]), lane_gain(a_k_norm[l]), lane_gain(b_q_norm[l]), lane_gain(b_k_norm[l]), seg)
            per_head = lambda p: p[l].reshape(B_HEADS, 1, HEAD_DIM)
            b_params = (slopes, per_head(lambda_q1), per_head(lambda_k1), per_head(lambda_q2), per_head(lambda_k2),
                        b_subln[l].reshape(B_VDIM, 1), lambda_init)

            bound_a = _score_bound(a_q_norm[l], a_k_norm[l])
            bound_b = _score_bound(b_q_norm[l], b_k_norm[l])

            def mixers(shift_path, qta=qta, ka=ka, vta=vta, qtb=qtb, kb=kb, vtb=vtb, b_params=b_params,
                       bound_a=bound_a, bound_b=bound_b):
                return (_attn_a_call(qta, ka, vta, bound_a, shift_path),
                        _attn_b_call(qtb, kb, vtb, bound_b, *b_params, shift_path))

            at, bt = lax.cond(jnp.maximum(bound_a, bound_b)[0, 0] <= SHIFT_PATH_MAX_BOUND,
                              lambda: mixers(True), lambda: mixers(False))
            xb = _out_call(xb, at, bt, bf(w_out[l]), ffn2_norm[l].reshape(1, D_MODEL),
                           bf(ffn2_w_gate[l]), bf(ffn2_w_up[l]), bf(ffn2_w_down[l]))
        outs.append(xb)
    return jnp.stack(outs, axis=0)
```

```python
import functools
import math

import jax
import jax.numpy as jnp
from jax import lax
from jax.experimental import pallas as pl
from jax.experimental.pallas import tpu as pltpu

D_MODEL = 1024
HEAD_DIM = 64
GRID_W = 64
EPS = 1e-6
ROPE_THETA = 10000.0
AX_DIM = HEAD_DIM // 2
A_HEADS = 8
A_KV_HEADS = 2
A_GROUP = A_HEADS // A_KV_HEADS
B_HEADS = 4
B_VDIM = 2 * HEAD_DIM
A_Q_COLS = A_HEADS * HEAD_DIM
A_KV_COLS = A_KV_HEADS * HEAD_DIM
B_QK_COLS = B_HEADS * 2 * HEAD_DIM
B_V_COLS = B_HEADS * B_VDIM
IN_COLS = A_Q_COLS + 2 * A_KV_COLS + 2 * B_QK_COLS + B_V_COLS
D_FF = 2816

LANES = 128
BF16_SUBLANES = 16
ROW_TILE = 512
KV_CHUNK = 512
A_Q_TOKENS = 128
B_Q_TOKENS = 256
TILES_PER_STEP = 2
FF_SPLITS = (0, 1536, D_FF)
VMEM_LIMIT = 56 * 1024 * 1024
NEG_BIG = -1e30

KA_LANES = 2 * HEAD_DIM
KB_LANES = 4 * HEAD_DIM
POS_SPLIT = 128
BOUND_MARGIN = 1.0 + 2.0 ** -6
SHIFT_PATH_MAX_BOUND = 40.0

_F32 = jnp.float32
_BF16 = jnp.bfloat16


def _dot(a, b):
    return jnp.dot(a, b, preferred_element_type=_F32)


def _rms_rows(x, gain):
    return x * lax.rsqrt(jnp.mean(x * x, axis=-1, keepdims=True) + EPS) * gain


def _swiglu_half_step(x, gain, wg_ref, wu_ref, wd_ref):
    h = _rms_rows(x, gain).astype(_BF16)
    acc = jnp.zeros(x.shape, _F32)
    for lo, hi in zip(FF_SPLITS[:-1], FF_SPLITS[1:]):
        cols = slice(lo, hi)
        g = _dot(h, wg_ref[:, cols])
        u = _dot(h, wu_ref[:, cols])
        a = (g / (1.0 + jnp.exp(-g)) * u).astype(_BF16)
        acc = acc + _dot(a, wd_ref[cols, :])
    return x + 0.5 * acc


def _ffn_body(x_ref, gain_ref, wg_ref, wu_ref, wd_ref, o_ref):
    o_ref[...] = _swiglu_half_step(x_ref[...], gain_ref[...], wg_ref, wu_ref, wd_ref)


def _chunk_sums(x, seg):
    hi = x.astype(_BF16)
    lo = (x - hi.astype(_F32)).astype(_BF16)
    return _dot(jnp.concatenate([hi, lo], axis=1), seg)


def _prep_body(x_ref, gain_ref, w_in_ref, cos_ref, sin_ref, gqa_ref, gka_ref, gqb_ref, gkb_ref, seg_ref,
               qta_ref, ka_ref, vta_ref, qtb_ref, kb_ref, vtb_ref):
    rows = x_ref.shape[0]
    h = _rms_rows(x_ref[...], gain_ref[...]).astype(_BF16)
    proj = _dot(h, w_in_ref[...])
    seg = seg_ref[...]
    scale = 1.0 / math.sqrt(HEAD_DIM)

    def block(col):
        return proj[:, col:col + LANES]

    def qk_norm(blk, gain):
        return blk * lax.rsqrt(_chunk_sums(blk * blk, seg) + EPS) * gain

    lane = lax.broadcasted_iota(jnp.int32, (rows, LANES), 1)
    first_half = (lane % AX_DIM) < (AX_DIM // 2)
    cos = cos_ref[...]
    sin = sin_ref[...]

    def rope(blk):
        partner = jnp.where(first_half, pltpu.roll(blk, LANES - AX_DIM // 2, 1), pltpu.roll(blk, AX_DIM // 2, 1))
        return blk * cos + partner * sin

    for b in range(A_Q_COLS // LANES):
        q = rope(qk_norm(block(b * LANES), gqa_ref[...])) * scale
        qta_ref[b * LANES:(b + 1) * LANES, :] = q.T.astype(_BF16)
    k = rope(qk_norm(block(A_Q_COLS), gka_ref[...])).astype(_BF16)
    vt = block(A_Q_COLS + A_KV_COLS).T.astype(_BF16)
    ones_rows = jnp.ones((BF16_SUBLANES, rows), _BF16)
    one_lane = (lax.broadcasted_iota(jnp.int32, (rows, HEAD_DIM), 1) == 0).astype(_BF16)
    for g in range(A_KV_HEADS):
        ka_ref[g] = jnp.concatenate([k[:, g * HEAD_DIM:(g + 1) * HEAD_DIM], one_lane], axis=1)
        vta_ref[g, 0, 0:HEAD_DIM, :] = vt[g * HEAD_DIM:(g + 1) * HEAD_DIM, :]
        vta_ref[g, 0, HEAD_DIM:HEAD_DIM + BF16_SUBLANES, :] = ones_rows
    pos = pl.program_id(0) * rows + lax.broadcasted_iota(jnp.int32, (rows, LANES), 0)
    pos_feat = jnp.where(lane == 0, (pos // POS_SPLIT * POS_SPLIT).astype(_F32),
                         jnp.where(lane == 1, (pos % POS_SPLIT).astype(_F32),
                                   jnp.where(lane < 5, 1.0, 0.0))).astype(_BF16)
    b_q0 = A_Q_COLS + 2 * A_KV_COLS
    b_k0 = b_q0 + B_QK_COLS
    b_v0 = b_k0 + B_QK_COLS
    for hd in range(B_HEADS):
        q = qk_norm(block(b_q0 + hd * LANES), gqb_ref[...]) * scale
        qtb_ref[hd * LANES:(hd + 1) * LANES, :] = q.T.astype(_BF16)
        kb_ref[:, hd * KB_LANES:hd * KB_LANES + LANES] = qk_norm(block(b_k0 + hd * LANES), gkb_ref[...]).astype(_BF16)
        kb_ref[:, hd * KB_LANES + LANES:(hd + 1) * KB_LANES] = pos_feat
        vtb_ref[hd, 0, 0:B_VDIM, :] = block(b_v0 + hd * LANES).T.astype(_BF16)
        vtb_ref[hd, 0, B_VDIM:B_VDIM + BF16_SUBLANES, :] = ones_rows


def _online_columns(w, k_ref, vt_ref, s_scr, p_scr, bias_fn):
    n = w.shape[1]
    rows = vt_ref.shape[1]
    nk = vt_ref.shape[0]
    assert nk % 2 == 0

    def scores(j, slot):
        k = k_ref[pl.ds(pl.multiple_of(j * KV_CHUNK, KV_CHUNK), KV_CHUNK), :]
        s = _dot(k, w)
        if bias_fn is not None:
            s = s - bias_fn(j)
        s_scr[slot] = s
        return jnp.max(s, axis=0, keepdims=True)

    def weights(slot, m, cmax):
        m_new = jnp.maximum(m, cmax)
        p_scr[slot] = jnp.exp(s_scr[slot] - m_new).astype(_BF16)
        return m_new, jnp.exp(m - m_new)

    def values(j, slot, acc, alpha):
        return alpha * acc + _dot(vt_ref[j], p_scr[slot])

    m = jnp.full((1, n), NEG_BIG, _F32)
    acc = jnp.zeros((rows, n), _F32)
    cmax0 = scores(0, 0)
    cmax1 = scores(1, 1)
    m, alpha = weights(0, m, cmax0)

    def body(i, carry):
        m, alpha, cmax1, acc = carry
        j = 2 * i + 1
        cmax0 = scores(j + 1, 0)
        acc = values(j - 1, 0, acc, alpha)
        m, alpha = weights(1, m, cmax1)
        cmax1 = scores(j + 2, 1)
        acc = values(j, 1, acc, alpha)
        m, alpha = weights(0, m, cmax0)
        return m, alpha, cmax1, acc

    m, alpha, cmax1, acc = lax.fori_loop(0, nk // 2 - 1, body, (m, alpha, cmax1, acc))
    acc = values(nk - 2, 0, acc, alpha)
    m, alpha = weights(1, m, cmax1)
    return values(nk - 1, 1, acc, alpha)


def _sublane_partial_sums(p):
    return jnp.sum(p.reshape(p.shape[0] // 8, 8, p.shape[1]), axis=0)


def _shifted_columns(chunk_fns, n_chunks, k_ref, vt_ref, s_scr, accs, denoms, finish):
    steps = [(t, c) for t in range(len(chunk_fns)) for c in range(n_chunks)]
    half = KV_CHUNK // 2
    halves = [slice(h * half, (h + 1) * half) for h in range(2)]

    def key_rows(j, rows):
        start = j * KV_CHUNK + rows.start
        return pl.ds(start if isinstance(j, int) else pl.multiple_of(start, half), half)

    def scores(t, c, slot, rows):
        j, w = chunk_fns[t](c)
        k = k_ref[key_rows(j, rows), :]
        s_scr[slot, rows, :] = jnp.minimum(_dot(k, w[0]), _dot(k, w[1])) if isinstance(w, tuple) else _dot(k, w)

    for rows in halves:
        scores(*steps[0], 0, rows)
    for i, (t, c) in enumerate(steps):
        j, _ = chunk_fns[t](c)
        for rows in halves:
            if i + 1 < len(steps):
                scores(*steps[i + 1], (i + 1) % 2, rows)
            p = jnp.exp(s_scr[i % 2, rows, :])
            if denoms[t] is not None:
                denoms[t] = denoms[t] + _sublane_partial_sums(p)
            accs[t] = accs[t] + _dot(vt_ref[j, 0:accs[t].shape[0], rows], p.astype(_BF16))
        if c == n_chunks - 1:
            finish(t, accs[t], denoms[t])


def _feature_rows(rows_list, n):
    row = lax.broadcasted_iota(jnp.int32, (BF16_SUBLANES, n), 0)
    tile = jnp.zeros((BF16_SUBLANES, n), _F32)
    for r, v in enumerate(rows_list):
        tile = jnp.where(row == r, v, tile)
    return tile.astype(_BF16)


def _tile_cols(t, tq):
    return slice(t * tq, (t + 1) * tq)


def _a_queries(qt_ref, cols):
    return jnp.concatenate([qt_ref[r * HEAD_DIM:(r + 1) * HEAD_DIM, cols] for r in range(A_GROUP)], axis=1)


def _a_finish(acc, o_ref, cols):
    tq = cols.stop - cols.start
    out = acc[0:HEAD_DIM, :] / acc[HEAD_DIM:HEAD_DIM + 1, :]
    for r in range(A_GROUP):
        o_ref[r * HEAD_DIM:(r + 1) * HEAD_DIM, cols] = out[:, r * tq:(r + 1) * tq].astype(o_ref.dtype)


def _attn_a_shift_body(qt_ref, k_ref, vt_ref, bound_ref, o_ref, s_scr):
    n = A_GROUP * A_Q_TOKENS
    feat = _feature_rows([-jnp.broadcast_to(bound_ref[...], (1, n))], n)
    pad = jnp.zeros((KA_LANES - HEAD_DIM - BF16_SUBLANES, n), _BF16)
    ws = [jnp.concatenate([_a_queries(qt_ref, _tile_cols(t, A_Q_TOKENS)), feat, pad], axis=0)
          for t in range(TILES_PER_STEP)]
    _shifted_columns([lambda c, w=w: (c, w) for w in ws], vt_ref.shape[0], k_ref, vt_ref, s_scr,
                     [jnp.zeros((vt_ref.shape[1], n), _F32)] * TILES_PER_STEP, [None] * TILES_PER_STEP,
                     lambda t, acc, _: _a_finish(acc, o_ref, _tile_cols(t, A_Q_TOKENS)))


def _attn_a_online_body(qt_ref, k_ref, vt_ref, o_ref, s_scr, p_scr):
    for t in range(TILES_PER_STEP):
        wq = _a_queries(qt_ref, _tile_cols(t, A_Q_TOKENS))
        w = jnp.concatenate([wq, jnp.zeros((KA_LANES - HEAD_DIM, wq.shape[1]), _BF16)], axis=0)
        _a_finish(_online_columns(w, k_ref, vt_ref, s_scr, p_scr, None), o_ref, _tile_cols(t, A_Q_TOKENS))


def _b_queries(qt_ref, cols):
    zeros = jnp.zeros((HEAD_DIM, cols.stop - cols.start), _BF16)
    return jnp.concatenate([jnp.concatenate([qt_ref[0:HEAD_DIM, cols], zeros], axis=1),
                            jnp.concatenate([zeros, qt_ref[HEAD_DIM:2 * HEAD_DIM, cols]], axis=1)], axis=0)


def _b_tile_start(t):
    return (pl.program_id(1) * TILES_PER_STEP + t) * B_Q_TOKENS


def _b_finish(acc, denom, lq1_ref, lk1_ref, lq2_ref, lk2_ref, subln_ref, o_ref, cols, lambda_init):
    tq = cols.stop - cols.start
    lam = (jnp.exp(jnp.sum(lq1_ref[...] * lk1_ref[...], axis=-1, keepdims=True))
           - jnp.exp(jnp.sum(lq2_ref[...] * lk2_ref[...], axis=-1, keepdims=True)) + lambda_init)
    o0 = acc[0:B_VDIM, 0:tq] / denom[:, 0:tq]
    o1 = acc[0:B_VDIM, tq:2 * tq] / denom[:, tq:2 * tq]
    o = o0 - lam * o1
    o = o * lax.rsqrt(jnp.mean(o * o, axis=0, keepdims=True) + EPS) * subln_ref[...] * (1.0 - lambda_init)
    o_ref[:, cols] = o.astype(o_ref.dtype)


def _attn_b_shift_body(qt_ref, k_ref, vt_ref, bound_ref, slope_ref, lq1_ref, lk1_ref, lq2_ref, lk2_ref,
                       subln_ref, o_ref, w_scr, s_scr, *, lambda_init):
    tq = B_Q_TOKENS
    n = 2 * tq
    bound = jnp.broadcast_to(bound_ref[...], (1, n))
    slope = jnp.broadcast_to(slope_ref[...], (1, n))
    pad = jnp.zeros((KB_LANES - 2 * HEAD_DIM - BF16_SUBLANES, n), _BF16)
    chunk_fns, accs, denoms = [], [], []
    for t in range(TILES_PER_STEP):
        wq = _b_queries(qt_ref, _tile_cols(t, tq))
        q_pos = _b_tile_start(t) + lax.broadcasted_iota(jnp.int32, (1, tq), 1)
        q_hi = (q_pos // POS_SPLIT * POS_SPLIT).astype(_F32)
        q_lo = (q_pos % POS_SPLIT).astype(_F32)
        sq_hi = slope * jnp.concatenate([q_hi, q_hi], axis=1)
        sq_lo = slope * jnp.concatenate([q_lo, q_lo], axis=1)
        before, after = 2 * t, 2 * t + 1
        w_scr[before] = jnp.concatenate([wq, _feature_rows([slope, slope, -sq_hi, -sq_lo, -bound], n), pad], axis=0)
        w_scr[after] = jnp.concatenate([wq, _feature_rows([-slope, -slope, sq_hi, sq_lo, -bound], n), pad], axis=0)
        accs.append(jnp.zeros((B_VDIM, n), _F32))
        denoms.append(jnp.zeros((8, n), _F32))
        diag = _b_tile_start(t) // KV_CHUNK

        def tile_chunk(c, diag=diag, before=before, after=after):
            if c == 0:
                return diag, (w_scr[before], w_scr[after])
            past = jnp.where(c - 1 >= diag, 1, 0)
            return c - 1 + past, w_scr[before + past]

        chunk_fns.append(tile_chunk)

    def finish(t, acc, denom):
        _b_finish(acc, jnp.sum(denom, axis=0, keepdims=True), lq1_ref, lk1_ref, lq2_ref, lk2_ref, subln_ref, o_ref,
                  _tile_cols(t, tq), lambda_init)

    _shifted_columns(chunk_fns, vt_ref.shape[0], k_ref, vt_ref, s_scr, accs, denoms, finish)


def _attn_b_online_body(qt_ref, k_ref, vt_ref, slope_ref, lq1_ref, lk1_ref, lq2_ref, lk2_ref, subln_ref, o_ref,
                        s_scr, p_scr, *, lambda_init):
    tq = B_Q_TOKENS
    slope = slope_ref[...]
    k_off = lax.broadcasted_iota(jnp.int32, (KV_CHUNK, tq), 0)
    for t in range(TILES_PER_STEP):
        wq = _b_queries(qt_ref, _tile_cols(t, tq))
        w = jnp.concatenate([wq, jnp.zeros((KB_LANES - 2 * HEAD_DIM, 2 * tq), _BF16)], axis=0)
        q_pos = _b_tile_start(t) + lax.broadcasted_iota(jnp.int32, (KV_CHUNK, tq), 1)

        def bias_fn(j, q_pos=q_pos):
            dist = jnp.abs(q_pos - (k_off + j * KV_CHUNK)).astype(_F32)
            bias = slope * dist
            return jnp.concatenate([bias, bias], axis=1)

        acc = _online_columns(w, k_ref, vt_ref, s_scr, p_scr, bias_fn)
        _b_finish(acc, acc[B_VDIM:B_VDIM + 1, :], lq1_ref, lk1_ref, lq2_ref, lk2_ref, subln_ref, o_ref,
                  _tile_cols(t, tq), lambda_init)


def _out_body(x_ref, at_ref, bt_ref, w_out_ref, gain_ref, wg_ref, wu_ref, wd_ref, o_ref):
    tn = (((0,), (0,)), ((), ()))
    half = at_ref.shape[0]
    y = (lax.dot_general(at_ref[...], w_out_ref[0:half, :], tn, preferred_element_type=_F32)
         + lax.dot_general(bt_ref[...], w_out_ref[half:2 * half, :], tn, preferred_element_type=_F32))
    x = x_ref[...] + y
    o_ref[...] = _swiglu_half_step(x, gain_ref[...], wg_ref, wu_ref, wd_ref)


def _resident(shape):
    return pl.BlockSpec(shape, lambda *_: (0,) * len(shape), pipeline_mode=pl.Buffered(1))


def _score_scratch(n):
    return pltpu.VMEM((2, KV_CHUNK, n), _F32)


def _online_scratch(n):
    return [_score_scratch(n), pltpu.VMEM((2, KV_CHUNK, n), _BF16)]


def _params(n_axes, fusible_inputs=None):
    return pltpu.CompilerParams(dimension_semantics=("arbitrary",) * n_axes, vmem_limit_bytes=VMEM_LIMIT,
                                allow_input_fusion=fusible_inputs)


def _ffn_call(x2d, gain, wg, wu, wd):
    s = x2d.shape[0]
    row = pl.BlockSpec((ROW_TILE, D_MODEL), lambda i: (i, 0))
    return pl.pallas_call(
        _ffn_body, name="ffn_half_step",
        grid=(s // ROW_TILE,),
        in_specs=[row, _resident((1, D_MODEL)), _resident(wg.shape), _resident(wu.shape), _resident(wd.shape)],
        out_specs=row,
        out_shape=jax.ShapeDtypeStruct((s, D_MODEL), _F32),
        compiler_params=_params(1, [False, False, True, True, True]),
    )(x2d, gain, wg, wu, wd)


def _prep_call(x2d, gain, w_in, cos, sin, gqa, gka, gqb, gkb, seg):
    s = x2d.shape[0]
    nk = s // KV_CHUNK
    assert ROW_TILE == KV_CHUNK
    row = lambda width: pl.BlockSpec((ROW_TILE, width), lambda i: (i, 0))
    col = lambda height: pl.BlockSpec((height, ROW_TILE), lambda i: (0, i))
    vec = _resident((1, LANES))
    out_shape = (
        jax.ShapeDtypeStruct((A_Q_COLS, s), _BF16),
        jax.ShapeDtypeStruct((A_KV_HEADS, s, KA_LANES), _BF16),
        jax.ShapeDtypeStruct((A_KV_HEADS, nk, HEAD_DIM + BF16_SUBLANES, KV_CHUNK), _BF16),
        jax.ShapeDtypeStruct((B_QK_COLS, s), _BF16),
        jax.ShapeDtypeStruct((s, B_HEADS * KB_LANES), _BF16),
        jax.ShapeDtypeStruct((B_HEADS, nk, B_VDIM + BF16_SUBLANES, KV_CHUNK), _BF16),
    )
    out_specs = (
        col(A_Q_COLS),
        pl.BlockSpec((A_KV_HEADS, ROW_TILE, KA_LANES), lambda i: (0, i, 0)),
        pl.BlockSpec((A_KV_HEADS, 1, HEAD_DIM + BF16_SUBLANES, KV_CHUNK), lambda i: (0, i, 0, 0)),
        col(B_QK_COLS),
        row(B_HEADS * KB_LANES),
        pl.BlockSpec((B_HEADS, 1, B_VDIM + BF16_SUBLANES, KV_CHUNK), lambda i: (0, i, 0, 0)),
    )
    return pl.pallas_call(
        _prep_body, name="in_proj_prep",
        grid=(s // ROW_TILE,),
        in_specs=[row(D_MODEL), _resident((1, D_MODEL)), _resident(w_in.shape), row(LANES), row(LANES),
                  vec, vec, vec, vec, _resident((2 * LANES, LANES))],
        out_specs=out_specs,
        out_shape=out_shape,
        compiler_params=_params(1, [False, False, True] + [False] * 7),
    )(x2d, gain, w_in, cos, sin, gqa, gka, gqb, gkb, seg)


def _attn_a_call(qta, ka, vta, bound, shift_path):
    s = qta.shape[1]
    nk = s // KV_CHUNK
    rows = A_GROUP * HEAD_DIM
    step_tokens = TILES_PER_STEP * A_Q_TOKENS
    in_specs = [pl.BlockSpec((rows, step_tokens), lambda g, i: (g, i)),
                pl.BlockSpec((None, s, KA_LANES), lambda g, i: (g, 0, 0)),
                pl.BlockSpec((None, nk, HEAD_DIM + BF16_SUBLANES, KV_CHUNK), lambda g, i: (g, 0, 0, 0))]
    common = dict(
        grid=(A_KV_HEADS, s // step_tokens),
        out_specs=pl.BlockSpec((rows, step_tokens), lambda g, i: (g, i)),
        out_shape=jax.ShapeDtypeStruct((A_Q_COLS, s), _BF16),
        compiler_params=_params(2))
    if shift_path:
        return pl.pallas_call(_attn_a_shift_body, name="attn_gqa_shift",
                              in_specs=in_specs + [pl.BlockSpec((1, 1), lambda g, i: (0, 0))],
                              scratch_shapes=[_score_scratch(A_GROUP * A_Q_TOKENS)], **common)(qta, ka, vta, bound)
    return pl.pallas_call(_attn_a_online_body, name="attn_gqa_online", in_specs=in_specs,
                          scratch_shapes=_online_scratch(A_GROUP * A_Q_TOKENS), **common)(qta, ka, vta)


def _attn_b_call(qtb, kb, vtb, bound, slopes, lq1, lk1, lq2, lk2, subln, lambda_init, shift_path):
    s = qtb.shape[1]
    nk = s // KV_CHUNK
    n = 2 * B_Q_TOKENS
    per_head = lambda width: pl.BlockSpec((None, 1, width), lambda h, i: (h, 0, 0))
    step_tokens = TILES_PER_STEP * B_Q_TOKENS
    qkv_specs = [pl.BlockSpec((2 * HEAD_DIM, step_tokens), lambda h, i: (h, i)),
                 pl.BlockSpec((s, KB_LANES), lambda h, i: (0, h)),
                 pl.BlockSpec((None, nk, B_VDIM + BF16_SUBLANES, KV_CHUNK), lambda h, i: (h, 0, 0, 0))]
    param_specs = [per_head(1), per_head(HEAD_DIM), per_head(HEAD_DIM), per_head(HEAD_DIM), per_head(HEAD_DIM),
                   pl.BlockSpec((B_VDIM, 1), lambda h, i: (0, 0))]
    common = dict(
        grid=(B_HEADS, s // step_tokens),
        out_specs=pl.BlockSpec((B_VDIM, step_tokens), lambda h, i: (h, i)),
        out_shape=jax.ShapeDtypeStruct((B_V_COLS, s), _BF16),
        compiler_params=_params(2))
    params = (slopes, lq1, lk1, lq2, lk2, subln)
    if shift_path:
        return pl.pallas_call(
            functools.partial(_attn_b_shift_body, lambda_init=lambda_init), name="attn_diff_shift",
            in_specs=qkv_specs + [pl.BlockSpec((1, 1), lambda h, i: (0, 0))] + param_specs,
            scratch_shapes=[pltpu.VMEM((2 * TILES_PER_STEP, KB_LANES, n), _BF16), _score_scratch(n)],
            **common)(qtb, kb, vtb, bound, *params)
    return pl.pallas_call(
        functools.partial(_attn_b_online_body, lambda_init=lambda_init), name="attn_diff_online",
        in_specs=qkv_specs + param_specs, scratch_shapes=_online_scratch(n), **common)(qtb, kb, vtb, *params)


def _out_call(x2d, at, bt, w_out, gain, wg, wu, wd):
    s = x2d.shape[0]
    row = pl.BlockSpec((ROW_TILE, D_MODEL), lambda i: (i, 0))
    col = pl.BlockSpec((at.shape[0], ROW_TILE), lambda i: (0, i))
    return pl.pallas_call(
        _out_body, name="out_proj_ffn",
        grid=(s // ROW_TILE,),
        in_specs=[row, col, col, _resident(w_out.shape), _resident((1, D_MODEL)),
                  _resident(wg.shape), _resident(wu.shape), _resident(wd.shape)],
        out_specs=row,
        out_shape=jax.ShapeDtypeStruct((s, D_MODEL), _F32),
        compiler_params=_params(1, [False, False, False, True, False, True, True, True]),
    )(x2d, at, bt, w_out, gain, wg, wu, wd)


def _rope_tables(s):
    rows = s // GRID_W
    row = jnp.broadcast_to(jnp.arange(rows, dtype=_F32)[:, None], (rows, GRID_W)).reshape(-1)
    colp = jnp.broadcast_to(jnp.arange(GRID_W, dtype=_F32)[None, :], (rows, GRID_W)).reshape(-1)
    inv_freq = ROPE_THETA ** (-jnp.arange(0, AX_DIM, 2, dtype=_F32) / AX_DIM)
    ang_r = row[:, None] * inv_freq[None, :]
    ang_c = colp[:, None] * inv_freq[None, :]
    cos = jnp.concatenate([jnp.cos(ang_r)] * 2 + [jnp.cos(ang_c)] * 2, axis=-1)
    sin = jnp.concatenate([-jnp.sin(ang_r), jnp.sin(ang_r), -jnp.sin(ang_c), jnp.sin(ang_c)], axis=-1)
    return jnp.tile(cos, (1, LANES // HEAD_DIM)), jnp.tile(sin, (1, LANES // HEAD_DIM))


def _score_bound(q_gain, k_gain):
    bound = math.sqrt(HEAD_DIM) * jnp.max(jnp.abs(q_gain)) * jnp.max(jnp.abs(k_gain)) * BOUND_MARGIN
    return bound.astype(_F32).reshape(1, 1)


def kernel(x, ffn1_norm, ffn1_w_gate, ffn1_w_up, ffn1_w_down, attn_norm, w_in, a_q_norm, a_k_norm, b_q_norm, b_k_norm, lambda_q1, lambda_k1, lambda_q2, lambda_k2, b_subln, w_out, ffn2_norm, ffn2_w_gate, ffn2_w_up, ffn2_w_down):
    bsz, s, _ = x.shape
    depth = w_in.shape[0]
    assert s % ROW_TILE == 0 and s % KV_CHUNK == 0 and s % (TILES_PER_STEP * B_Q_TOKENS) == 0
    assert KV_CHUNK % B_Q_TOKENS == 0 and s < POS_SPLIT * 256
    cos, sin = _rope_tables(s)
    lane_chunk = jnp.arange(LANES) // HEAD_DIM
    seg = ((lane_chunk[:, None] == lane_chunk[None, :]).astype(_F32) / HEAD_DIM).astype(_BF16)
    seg = jnp.concatenate([seg, seg], axis=0)
    slopes = (2.0 ** (-8.0 * (jnp.arange(B_HEADS, dtype=_F32) + 1.0) / B_HEADS)).reshape(B_HEADS, 1, 1)
    lane_gain = lambda g: jnp.tile(g, LANES // HEAD_DIM).reshape(1, LANES)
    bf = lambda w: w.astype(_BF16)

    outs = []
    for b in range(bsz):
        xb = x[b]
        for l in range(depth):
            lambda_init = 0.8 - 0.6 * math.exp(-0.3 * l)
            xb = _ffn_call(xb, ffn1_norm[l].reshape(1, D_MODEL), bf(ffn1_w_gate[l]), bf(ffn1_w_up[l]), bf(ffn1_w_down[l]))
            qta, ka, vta, qtb, kb, vtb = _prep_call(
                xb, attn_norm[l].reshape(1, D_MODEL), bf(w_in[l]), cos, sin,
                lane_gain(a_q_norm[l]), lane_gain(a_k_norm[l]), lane_gain(b_q_norm[l]), lane_gain(b_k_norm[l]), seg)
            per_head = lambda p: p[l].reshape(B_HEADS, 1, HEAD_DIM)
            b_params = (slopes, per_head(lambda_q1), per_head(lambda_k1), per_head(lambda_q2), per_head(lambda_k2),
                        b_subln[l].reshape(B_VDIM, 1), lambda_init)

            bound_a = _score_bound(a_q_norm[l], a_k_norm[l])
            bound_b = _score_bound(b_q_norm[l], b_k_norm[l])

            def mixers(shift_path, qta=qta, ka=ka, vta=vta, qtb=qtb, kb=kb, vtb=vtb, b_params=b_params,
                       bound_a=bound_a, bound_b=bound_b):
                return (_attn_a_call(qta, ka, vta, bound_a, shift_path),
                        _attn_b_call(qtb, kb, vtb, bound_b, *b_params, shift_path))

            at, bt = lax.cond(jnp.maximum(bound_a, bound_b)[0, 0] <= SHIFT_PATH_MAX_BOUND,
                              lambda: mixers(True), lambda: mixers(False))
            xb = _out_call(xb, at, bt, bf(w_out[l]), ffn2_norm[l].reshape(1, D_MODEL),
                           bf(ffn2_w_gate[l]), bf(ffn2_w_up[l]), bf(ffn2_w_down[l]))
        outs.append(xb)
    return jnp.stack(outs, axis=0)
```
